```python
import jax, jax.numpy as jnp
from jax import lax
import numpy as np

D_MODEL = 1024
BATCH = 8
SEQ = 4096
DEPTH = 1

D_MIX = D_MODEL
RWKV_HEAD = 64
RWKV_HEADS = 8
RWKV_WIDTH = RWKV_HEADS * RWKV_HEAD
DECAY_LORA = 64
ICLR_LORA = 64
GATE_LORA = 128
MLA_HEADS = 8
MLA_NOPE = 64
MLA_ROPE = 32
MLA_V = 64
MLA_WIDTH = MLA_HEADS * MLA_V
Q_LORA = 384
KV_LORA = 256
ROPE_BASE = 10000.0
ATTN_BLOCK = 128
D_FF = 4 * D_MODEL
ALPHA = (2 * DEPTH) ** 0.25
BETA = (8 * DEPTH) ** -0.25
LN_EPS = 1e-5
RMS_EPS = 1e-6
GN_EPS = 64e-5
RWKV_SPLITS = [RWKV_WIDTH, RWKV_WIDTH, RWKV_WIDTH, DECAY_LORA, ICLR_LORA, GATE_LORA]
MLA_SPLITS = [Q_LORA, KV_LORA, MLA_ROPE]
RWKV_COLS = sum(RWKV_SPLITS)
MLA_COLS = sum(MLA_SPLITS)
IN_COLS = RWKV_COLS + MLA_COLS

kernel_name = "hymba_rwkv7_mla_deepnorm_layer"


def _split(z, sizes):
    return jnp.split(z, list(np.cumsum(sizes)[:-1]), axis=-1)


def layer_norm(x, g, b):
    xf = x.astype(jnp.float32)
    mu = jnp.mean(xf, axis=-1, keepdims=True)
    var = jnp.mean(jnp.square(xf - mu), axis=-1, keepdims=True)
    return ((xf - mu) * lax.rsqrt(var + LN_EPS)).astype(x.dtype) * g + b


def rms_norm(x, g):
    xf = x.astype(jnp.float32)
    return (xf * lax.rsqrt(jnp.mean(xf * xf, axis=-1, keepdims=True) + RMS_EPS)).astype(x.dtype) * g


def rope(z, ang):
    cos, sin = jnp.cos(ang), jnp.sin(ang)
    zf = z.astype(jnp.float32)
    z1, z2 = jnp.split(zf, 2, axis=-1)
    return jnp.concatenate([z1 * cos - z2 * sin, z2 * cos + z1 * sin], axis=-1).astype(z.dtype)


def token_shift(z):
    return jnp.pad(z, ((0, 0), (1, 0), (0, 0)))[:, :-1]


def wkv7_scan(r, w, k, v, kk, a):
    B, _, H, N = r.shape
    tm = lambda t: jnp.moveaxis(t.astype(jnp.float32), 1, 0)

    def step(S, inp):
        r_t, w_t, k_t, v_t, kk_t, a_t = inp
        sa = jnp.einsum('bhvk,bhk->bhv', S, -kk_t)
        S = (S * w_t[:, :, None, :]
             + sa[..., :, None] * (kk_t * a_t)[:, :, None, :]
             + v_t[..., :, None] * k_t[:, :, None, :])
        y = jnp.einsum('bhvk,bhk->bhv', S, r_t)
        return S, y

    S0 = jnp.zeros((B, H, N, N), jnp.float32)
    _, y = lax.scan(step, S0, (tm(r), tm(w), tm(k), tm(v), tm(kk), tm(a)))
    return jnp.moveaxis(y, 0, 1).astype(r.dtype)


def rwkv7_group(hr, shift_mu, decay_w0, decay_w2, iclr_a0, iclr_a2, gate_g2, k_k, k_a, r_k, lnx_g, lnx_b):
    B, S, _ = hr.shape
    H, N = RWKV_HEADS, RWKV_HEAD
    hr = hr + (token_shift(hr) - hr) * shift_mu
    r, k, v, xw, xa, xg = _split(hr, RWKV_SPLITS)
    w_log = -jax.nn.softplus(-(decay_w0 + jnp.tanh(xw) @ decay_w2)) - 0.5
    decay = jnp.exp(-jnp.exp(w_log.astype(jnp.float32)))
    a = jax.nn.sigmoid(iclr_a0 + xa @ iclr_a2)
    g = jax.nn.sigmoid(xg) @ gate_g2
    heads = lambda t: t.reshape(B, S, H, N)
    kk = heads(k * k_k).astype(jnp.float32)
    kk = kk / jnp.maximum(jnp.sqrt(jnp.sum(kk * kk, axis=-1, keepdims=True)), 1e-12)
    k = k * (1.0 + (a - 1.0) * k_a)
    rh, kh, vh, ah = heads(r), heads(k), heads(v), heads(a)
    y = wkv7_scan(rh, heads(decay), kh, vh, kk, ah)
    yf = y.astype(jnp.float32)
    mu = jnp.mean(yf, axis=-1, keepdims=True)
    var = jnp.mean(jnp.square(yf - mu), axis=-1, keepdims=True)
    yn = ((yf - mu) * lax.rsqrt(var + GN_EPS)).astype(y.dtype).reshape(B, S, RWKV_WIDTH) * lnx_g + lnx_b
    bonus = (jnp.sum(rh * kh * r_k, axis=-1, keepdims=True) * vh).reshape(B, S, RWKV_WIDTH)
    return (yn + bonus) * g


def causal_block_attention(q, k, v):
    B, S, H, Dq = q.shape
    Dv = v.shape[-1]
    nb = S // ATTN_BLOCK
    scale = Dq ** -0.5
    qb = jnp.moveaxis(q.reshape(B, nb, ATTN_BLOCK, H, Dq), 1, 0)
    kpos = jnp.arange(S)

    def one_block(args):
        q_blk, i = args
        s = jnp.einsum('bqhd,bkhd->bhqk', q_blk, k).astype(jnp.float32) * scale
        qpos = i * ATTN_BLOCK + jnp.arange(ATTN_BLOCK)
        s = jnp.where(kpos[None, :] <= qpos[:, None], s, -jnp.inf)
        p = jax.nn.softmax(s, axis=-1).astype(v.dtype)
        return jnp.einsum('bhqk,bkhd->bqhd', p, v)

    o = lax.map(one_block, (qb, jnp.arange(nb)))
    return jnp.moveaxis(o, 0, 1).reshape(B, S, H * Dv)


def mla_group(hm, positions, q_norm_g, w_uq, kv_norm_g, w_ukv):
    B, S, _ = hm.shape
    H = MLA_HEADS
    c_q, c_kv, k_pe = _split(hm, MLA_SPLITS)
    inv_freq = ROPE_BASE ** (-jnp.arange(0, MLA_ROPE, 2, dtype=jnp.float32) / MLA_ROPE)
    ang = positions.astype(jnp.float32)[..., None] * inv_freq
    q = (rms_norm(c_q, q_norm_g) @ w_uq).reshape(B, S, H, MLA_NOPE + MLA_ROPE)
    q_nope, q_pe = _split(q, [MLA_NOPE, MLA_ROPE])
    q_pe = rope(q_pe, ang[:, :, None, :])
    kv = (rms_norm(c_kv, kv_norm_g) @ w_ukv).reshape(B, S, H, MLA_NOPE + MLA_V)
    k_nope, v = _split(kv, [MLA_NOPE, MLA_V])
    k_pe = rope(k_pe, ang)
    q = jnp.concatenate([q_nope, q_pe], axis=-1)
    k = jnp.concatenate([k_nope, jnp.broadcast_to(k_pe[:, :, None, :], (B, S, H, MLA_ROPE))], axis=-1)
    return causal_block_attention(q, k, v)


def setup_inputs(seed: int = 0) -> dict:
    key = jax.random.key(seed)
    ks = jax.random.split(key, 26)
    f32 = jnp.float32
    nrm = lambda k, shape, s: jax.random.normal(k, shape, f32) * s
    L = DEPTH
    x = jax.random.normal(ks[0], (BATCH, SEQ, D_MODEL), f32)
    offset = jax.random.randint(ks[1], (BATCH, 1), 0, 4096, dtype=jnp.int32)
    positions = (offset + jnp.arange(SEQ, dtype=jnp.int32)[None, :]).astype(jnp.int32)
    return {
        "x": x,
        "positions": positions,
        "w_in": nrm(ks[2], (L, D_MODEL, IN_COLS), D_MODEL ** -0.5),
        "shift_mu": jax.random.uniform(ks[3], (L, RWKV_COLS), f32),
        "decay_w0": jax.random.uniform(ks[4], (L, RWKV_WIDTH), f32, -5.0, 1.0),
        "decay_w2": nrm(ks[5], (L, DECAY_LORA, RWKV_WIDTH), 0.1),
        "iclr_a0": nrm(ks[6], (L, RWKV_WIDTH), 0.5),
        "iclr_a2": nrm(ks[7], (L, ICLR_LORA, RWKV_WIDTH), 0.1),
        "gate_g2": nrm(ks[8], (L, GATE_LORA, RWKV_WIDTH), GATE_LORA ** -0.5),
        "k_k": 0.85 + nrm(ks[9], (L, RWKV_WIDTH), 0.05),
        "k_a": 1.0 + nrm(ks[10], (L, RWKV_WIDTH), 0.05),
        "r_k": nrm(ks[11], (L, RWKV_HEADS, RWKV_HEAD), 0.1),
        "lnx_g": 1.0 + nrm(ks[12], (L, RWKV_WIDTH), 0.1),
        "lnx_b": nrm(ks[13], (L, RWKV_WIDTH), 0.01),
        "q_norm_g": 1.0 + nrm(ks[14], (L, Q_LORA), 0.1),
        "w_uq": nrm(ks[15], (L, Q_LORA, MLA_HEADS * (MLA_NOPE + MLA_ROPE)), Q_LORA ** -0.5),
        "kv_norm_g": 1.0 + nrm(ks[16], (L, KV_LORA), 0.1),
        "w_ukv": nrm(ks[17], (L, KV_LORA, MLA_HEADS * (MLA_NOPE + MLA_V)), KV_LORA ** -0.5),
        "w_out": nrm(ks[18], (L, D_MIX, D_MODEL), BETA * D_MIX ** -0.5),
        "ln1_g": 1.0 + nrm(ks[19], (L, D_MODEL), 0.1),
        "ln1_b": nrm(ks[20], (L, D_MODEL), 0.01),
        "w_ffn1": nrm(ks[21], (L, D_MODEL, D_FF), D_MODEL ** -0.5),
        "w_ffn2": nrm(ks[22], (L, D_FF, D_MODEL), BETA * D_FF ** -0.5),
        "ln2_g": 1.0 + nrm(ks[23], (L, D_MODEL), 0.1),
        "ln2_b": nrm(ks[24], (L, D_MODEL), 0.01),
    }


def reference(x, positions, w_in, shift_mu, decay_w0, decay_w2, iclr_a0, iclr_a2, gate_g2, k_k, k_a, r_k,
              lnx_g, lnx_b, q_norm_g, w_uq, kv_norm_g, w_ukv, w_out, ln1_g, ln1_b, w_ffn1, w_ffn2,
              ln2_g, ln2_b):
    for l in range(DEPTH):
        h = x @ w_in[l]
        y_rwkv = rwkv7_group(h[..., :RWKV_COLS], shift_mu[l], decay_w0[l], decay_w2[l], iclr_a0[l],
                             iclr_a2[l], gate_g2[l], k_k[l], k_a[l], r_k[l], lnx_g[l], lnx_b[l])
        y_mla = mla_group(h[..., RWKV_COLS:], positions, q_norm_g[l], w_uq[l], kv_norm_g[l], w_ukv[l])
        mix = jnp.concatenate([y_rwkv, y_mla], axis=-1) @ w_out[l]
        x = layer_norm(ALPHA * x + mix, ln1_g[l], ln1_b[l])
        f = jnp.square(jax.nn.relu(x @ w_ffn1[l])) @ w_ffn2[l]
        x = layer_norm(ALPHA * x + f, ln2_g[l], ln2_b[l])
    return x
```

```python
import functools
import math

import jax
import jax.numpy as jnp
from jax import lax
from jax.experimental import pallas as pl
from jax.experimental.pallas import tpu as pltpu

F32 = jnp.float32
BF16 = jnp.bfloat16

D_MODEL = 1024
RWKV_HEAD = 64
RWKV_HEADS = 8
RWKV_WIDTH = RWKV_HEADS * RWKV_HEAD
DECAY_LORA = 64
ICLR_LORA = 64
GATE_LORA = 128
MLA_HEADS = 8
MLA_NOPE = 64
MLA_ROPE = 32
MLA_V = 64
MLA_WIDTH = MLA_HEADS * MLA_V
Q_LORA = 384
KV_LORA = 256
ROPE_BASE = 10000.0
D_FF = 4 * D_MODEL
LN_EPS = 1e-5
RMS_EPS = 1e-6
GN_EPS = 64e-5
RWKV_COLS = 3 * RWKV_WIDTH + DECAY_LORA + ICLR_LORA + GATE_LORA
MLA_COLS = Q_LORA + KV_LORA + MLA_ROPE

LANES = 128
VMEM_LIMIT = 56 * 1024 * 1024

C_R, C_K, C_V = 0, RWKV_WIDTH, 2 * RWKV_WIDTH
C_WA = 3 * RWKV_WIDTH
C_G = C_WA + DECAY_LORA + ICLR_LORA
C_Q = RWKV_COLS
C_KV = C_Q + Q_LORA
C_PE = C_KV + KV_LORA
IN_COLS_PAD = C_PE + LANES
PE_LO = MLA_NOPE
PE_HALF = MLA_ROPE // 2

CHUNK = 64


def _dot(a, b):
    return jnp.dot(a.astype(BF16), b.astype(BF16), preferred_element_type=F32)


def _split_bf16(x):
    hi = x.astype(BF16)
    lo = (x - hi.astype(F32)).astype(BF16)
    return hi, lo


def _dot_left_split(x, w_exact):
    hi, lo = _split_bf16(x)
    return (jnp.dot(hi, w_exact, preferred_element_type=F32)
            + jnp.dot(lo, w_exact, preferred_element_type=F32))


def _dot3(a, b, dims):
    ah, al = _split_bf16(a)
    bh, bl = _split_bf16(b)
    d = functools.partial(lax.dot_general, dimension_numbers=dims, preferred_element_type=F32)
    return d(ah, bh) + (d(ah, bl) + d(al, bh))


def _sigmoid(z):
    return 1.0 / (1.0 + jnp.exp(-z))


def _layer_norm(x, g, b):
    mu = jnp.mean(x, axis=-1, keepdims=True)
    d = x - mu
    var = jnp.mean(d * d, axis=-1, keepdims=True)
    return d * lax.rsqrt(var + LN_EPS) * g + b


def _prep_kernel(x_ref, pos_ref, w_ref, mu_ref, lora_ref, w0a0_ref, g2_ref, qg_ref, wq_ref,
                 kvg_ref, wk_ref, wvt_ref, freq_ref, sgn_ref,
                 r_ref, k_ref, v_ref, lw_ref, a_ref, g_ref, q_ref, kh_ref, vt_ref, carry_ref,
                 *, scale):
    tm = x_ref.shape[0]
    j = pl.program_id(1)

    @pl.when(j == 0)
    def _():
        carry_ref[...] = jnp.zeros_like(carry_ref)

    h = jnp.dot(x_ref[...].astype(BF16), w_ref[...], preferred_element_type=F32)

    hr = h[:, :RWKV_COLS]
    row = lax.broadcasted_iota(jnp.int32, hr.shape, 0)
    prev = jnp.where(row == 0, carry_ref[0:1, :], pltpu.roll(hr, 1, axis=0))
    carry_ref[0:1, :] = hr[tm - 1:tm, :]
    hr = hr + (prev - hr) * mu_ref[...]

    r_ref[...] = hr[:, C_R:C_R + RWKV_WIDTH]
    k_ref[...] = hr[:, C_K:C_K + RWKV_WIDTH]
    v_ref[...] = hr[:, C_V:C_V + RWKV_WIDTH]

    lane = lax.broadcasted_iota(jnp.int32, (tm, LANES), 1)
    xwa = hr[:, C_WA:C_WA + LANES]
    xwa = jnp.where(lane < DECAY_LORA, jnp.tanh(xwa), xwa)
    z = _dot(xwa, lora_ref[...]) + w0a0_ref[...]
    sg = _sigmoid(z)
    lw_ref[...] = (-math.exp(-0.5)) * sg[:, :RWKV_WIDTH]
    a_ref[...] = sg[:, RWKV_WIDTH:]
    g_ref[...] = _dot(_sigmoid(hr[:, C_G:C_G + GATE_LORA]), g2_ref[...])

    def rms(c, gain):
        return c * lax.rsqrt(jnp.mean(c * c, axis=-1, keepdims=True) + RMS_EPS) * gain

    cqn = rms(h[:, C_Q:C_Q + Q_LORA], qg_ref[...]).astype(BF16)
    ckvn = rms(h[:, C_KV:C_KV + KV_LORA], kvg_ref[...]).astype(BF16)
    q_all = jnp.dot(cqn, wq_ref[...], preferred_element_type=F32)
    k_all = jnp.dot(ckvn, wk_ref[...], preferred_element_type=F32)
    vt = lax.dot_general(wvt_ref[...], ckvn, (((1,), (1,)), ((), ())),
                         preferred_element_type=F32)
    vt_ref[...] = vt.astype(vt_ref.dtype)

    ang = pos_ref[...].astype(F32) * freq_ref[...]
    cosf = jnp.cos(ang)
    sins = jnp.sin(ang) * sgn_ref[...]
    first_half = (lane >= PE_LO) & (lane < PE_LO + PE_HALF)

    def rope(zg):
        rot = jnp.where(first_half, pltpu.roll(zg, LANES - PE_HALF, axis=1),
                        pltpu.roll(zg, PE_HALF, axis=1))
        return zg * cosf + rot * sins

    kpe = rope(h[:, C_PE:C_PE + LANES])
    for hd in range(MLA_HEADS):
        sl = slice(hd * LANES, (hd + 1) * LANES)
        q_ref[hd] = (rope(q_all[:, sl]) * scale).astype(q_ref.dtype)
        kh_ref[hd] = (k_all[:, sl] + kpe).astype(kh_ref.dtype)


_NN = (((2,), (1,)), ((0,), (0,)))
_NT = (((2,), (2,)), ((0,), (0,)))
_TN = (((1,), (1,)), ((0,), (0,)))


def _bdot(a, b, dims=_NN):
    return lax.dot_general(a.astype(BF16), b.astype(BF16), dims, preferred_element_type=F32)


def _rwkv_kernel(r_ref, k_ref, v_ref, lw_ref, a_ref, g_ref, par_ref, o_ref, h_ref, *, chunk):
    tb = r_ref.shape[0]
    C = chunk
    nc = tb // C
    C2 = 2 * C
    j = pl.program_id(2)

    @pl.when(j == 0)
    def _():
        h_ref[...] = jnp.zeros_like(h_ref)

    lane = lax.broadcasted_iota(jnp.int32, (1, LANES), 1)
    m0 = (lane < RWKV_HEAD).astype(F32)
    m1 = 1.0 - m0
    rr = lax.broadcasted_iota(jnp.int32, (LANES, LANES), 0)
    cc = lax.broadcasted_iota(jnp.int32, (LANES, LANES), 1)
    ones2 = ((rr < RWKV_HEAD) == (cc < RWKV_HEAD)).astype(BF16)
    eye = (rr == cc).astype(F32)

    k_k, k_a, r_k = par_ref[0:1, :], par_ref[1:2, :], par_ref[2:3, :]
    ln_g, ln_b = par_ref[3:4, :], par_ref[4:5, :]

    r = r_ref[...]
    k = k_ref[...]
    v = v_ref[...]
    lw = lw_ref[...]
    a = a_ref[...]

    kk0 = k * k_k
    n2 = _dot_left_split(kk0 * kk0, ones2)
    kk = kk0 / jnp.maximum(jnp.sqrt(n2), 1e-12)
    kmod = k * (1.0 + (a - 1.0) * k_a)
    bonus = _dot_left_split(r * kmod * r_k, ones2) * v
    bb = kk * a

    to3 = lambda t: t.reshape(nc, C, LANES)
    r3, k3, v3, lw3, kk3, bb3 = to3(r), to3(kmod), to3(v), to3(lw), to3(kk), to3(bb)

    ti = lax.broadcasted_iota(jnp.int32, (nc, C, C), 1)
    si = lax.broadcasted_iota(jnp.int32, (nc, C, C), 2)
    tril = (si <= ti).astype(BF16)
    lhi, llo = _split_bf16(lw3)
    cum = (lax.dot_general(tril, lhi, _NN, preferred_element_type=F32)
           + lax.dot_general(tril, llo, _NN, preferred_element_type=F32))
    cend = cum[:, C - 1:C, :]
    p_inc = jnp.exp(cum)
    p_exc = jnp.exp(cum - lw3)
    p_inv = jnp.exp(-cum)
    p_end = jnp.exp(cend - cum)
    p_all = jnp.exp(cend)

    expand = lambda t: jnp.concatenate([t * m0, t * m1], axis=1)
    at2 = expand(-kk3 * p_exc)
    rt2 = expand(r3 * p_inc)
    bt2 = expand(bb3 * p_inv)
    kt2 = expand(k3 * p_inv)
    v2 = expand(v3)
    bbar2 = expand(bb3 * p_end)
    kbar2 = expand(k3 * p_end)

    lhs = jnp.concatenate([at2, rt2], axis=1)
    rhs = jnp.concatenate([bt2, kt2], axis=1)
    m_all = _dot3(lhs, rhs, _NT)
    ri = lax.broadcasted_iota(jnp.int32, (4 * C, 4 * C), 0)
    ci = lax.broadcasted_iota(jnp.int32, (4 * C, 4 * C), 1)
    bottom = (ri >= C2).astype(jnp.int32)
    right = (ci >= C2).astype(jnp.int32)
    keep = (ri - C2 * bottom) - (ci - C2 * right) + bottom > 0
    m_all = jnp.where(keep[None], m_all, 0.0)
    a_ab = m_all[:, :C2, :C2]
    a_ak = m_all[:, :C2, C2:]
    a_r = m_all[:, C2:, :]

    pw = a_ab
    tmat = eye[None] + a_ab
    for _ in range(max(1, (C - 1).bit_length()) - 1):
        pw = _dot3(pw, pw, _NN)
        tmat = tmat + _dot3(pw, tmat, _NN)

    akv = _dot3(a_ak, v2, _NN)
    wu = _dot3(tmat, jnp.concatenate([at2, akv], axis=2), _NN)
    zmat = jnp.concatenate(
        [wu, jnp.concatenate([jnp.zeros_like(v2), v2], axis=2)], axis=1)
    ry = _dot3(a_r, zmat, _NN)
    rq2 = rt2 + ry[:, :, :LANES]
    yadd2 = ry[:, :, LANES:]
    bk = jnp.concatenate([bbar2, kbar2], axis=1)
    gh = _dot3(bk, zmat, _TN)
    g2 = gh[:, :, :LANES] + eye[None] * p_all
    hadd2 = gh[:, :, LANES:]

    hst = h_ref[...]
    ys = []
    for c in range(nc):
        y2 = _dot3(rq2[c], hst, (((1,), (0,)), ((), ()))) + yadd2[c]
        ys.append(y2[:C] + y2[C:])
        hst = _dot3(g2[c], hst, (((1,), (0,)), ((), ()))) + hadd2[c]
    h_ref[...] = hst
    y = jnp.concatenate(ys, axis=0)

    inv_n = 1.0 / RWKV_HEAD
    mu = _dot_left_split(y, ones2) * inv_n
    d = y - mu
    var = _dot_left_split(d * d, ones2) * inv_n
    yn = d * lax.rsqrt(var + GN_EPS) * ln_g + ln_b
    o_ref[...] = ((yn + bonus) * g_ref[...]).astype(o_ref.dtype)


def _attn_kernel(q_ref, k_ref, vt_ref, o_ref):
    tq = q_ref.shape[0]
    tk = vt_ref.shape[2]
    i = pl.program_id(2)
    q = q_ref[...]
    dn = (((1,), (1,)), ((), ()))

    def block(jb, carry, masked):
        m, l, acc = carry
        kb = k_ref[pl.ds(pl.multiple_of(jb * tk, tk), tk), :]
        st = lax.dot_general(kb, q, dn, preferred_element_type=F32)
        if masked:
            kpos = jb * tk + lax.broadcasted_iota(jnp.int32, st.shape, 0)
            qpos = i * tq + lax.broadcasted_iota(jnp.int32, st.shape, 1)
            st = jnp.where(kpos <= qpos, st, -jnp.inf)
        mn = jnp.maximum(m, jnp.max(st, axis=0, keepdims=True))
        alpha = jnp.exp(m - mn)
        p = jnp.exp(st - mn)
        l = l * alpha + jnp.sum(p, axis=0, keepdims=True)
        acc = acc * alpha + jnp.dot(vt_ref[jb], p.astype(BF16), preferred_element_type=F32)
        return mn, l, acc

    init = (jnp.full((1, tq), -jnp.inf, F32), jnp.zeros((1, tq), F32),
            jnp.zeros((MLA_V, tq), F32))
    nfull = (i * tq) // tk
    carry = lax.fori_loop(0, nfull, lambda jb, c: block(jb, c, False), init)
    for d in range(tq // tk):
        carry = block(nfull + d, carry, True)
    _, l, acc = carry
    o_ref[...] = (acc / l).astype(o_ref.dtype)


def _out_kernel(x_ref, yr_ref, ymt_ref, wor_ref, wom_ref, ln1_ref, w1_ref, w2_ref, ln2_ref, o_ref,
                *, alpha, ff_chunk):
    x = x_ref[...]
    mix = jnp.dot(yr_ref[...], wor_ref[...], preferred_element_type=F32)
    mix = mix + lax.dot_general(ymt_ref[...], wom_ref[...], (((0,), (0,)), ((), ())),
                                preferred_element_type=F32)
    x1 = _layer_norm(alpha * x + mix, ln1_ref[0:1, :], ln1_ref[1:2, :])
    x1b = x1.astype(BF16)
    f = jnp.zeros_like(x1)
    for c in range(0, w1_ref.shape[1], ff_chunk):
        hid = jnp.dot(x1b, w1_ref[:, c:c + ff_chunk], preferred_element_type=F32)
        hid = jnp.square(jnp.maximum(hid, 0.0)).astype(BF16)
        f = f + jnp.dot(hid, w2_ref[c:c + ff_chunk, :], preferred_element_type=F32)
    o_ref[...] = _layer_norm(alpha * x1 + f, ln2_ref[0:1, :], ln2_ref[1:2, :])


def _const_spec(shape):
    nd = len(shape)
    return pl.BlockSpec(shape, lambda *_: (0,) * nd, pipeline_mode=pl.Buffered(1))


def _pick(n, pref):
    t = min(pref, n)
    while n % t:
        t //= 2
    return t


def _layer(x, positions, w_in, shift_mu, decay_w0, decay_w2, iclr_a0, iclr_a2, gate_g2, k_k, k_a,
           r_k, lnx_g, lnx_b, q_norm_g, w_uq, kv_norm_g, w_ukv, w_out, ln1_g, ln1_b, w_ffn1,
           w_ffn2, ln2_g, ln2_b, alpha):
    B, S, D = x.shape
    T = B * S
    H = MLA_HEADS
    tm = _pick(S, 512)
    tq = _pick(S, 512)
    tb = _pick(S, 512)
    chunk = _pick(tb, CHUNK)
    nt = S // tm

    w_r = w_in[:, :RWKV_COLS]
    w_m = w_in[:, RWKV_COLS:]
    w_pe = jnp.zeros((D, LANES), F32).at[:, PE_LO:PE_LO + MLA_ROPE].set(w_m[:, Q_LORA + KV_LORA:])
    w_all = jnp.concatenate([w_r, w_m[:, :Q_LORA + KV_LORA], w_pe], axis=1).astype(BF16)
    mu = shift_mu.reshape(1, RWKV_COLS)
    lora = jnp.zeros((LANES, 2 * RWKV_WIDTH), F32)
    lora = lora.at[:DECAY_LORA, :RWKV_WIDTH].set(decay_w2).at[DECAY_LORA:, RWKV_WIDTH:].set(iclr_a2)
    lora = lora.astype(BF16)
    w0a0 = jnp.concatenate([decay_w0, iclr_a0]).reshape(1, 2 * RWKV_WIDTH)
    g2 = gate_g2.astype(BF16)
    dq = MLA_NOPE + MLA_ROPE
    wq = jnp.zeros((Q_LORA, H, LANES), F32).at[:, :, :dq].set(w_uq.reshape(Q_LORA, H, dq))
    wq = wq.reshape(Q_LORA, H * LANES).astype(BF16)
    wkv = w_ukv.reshape(KV_LORA, H, MLA_NOPE + MLA_V)
    wk = jnp.zeros((KV_LORA, H, LANES), F32).at[:, :, :MLA_NOPE].set(wkv[:, :, :MLA_NOPE])
    wk = wk.reshape(KV_LORA, H * LANES).astype(BF16)
    wvt = wkv[:, :, MLA_NOPE:].reshape(KV_LORA, H * MLA_V).T.astype(BF16)
    inv_freq = ROPE_BASE ** (-jnp.arange(0, MLA_ROPE, 2, dtype=F32) / MLA_ROPE)
    freq = jnp.zeros((1, LANES), F32).at[0, PE_LO:PE_LO + PE_HALF].set(inv_freq)
    freq = freq.at[0, PE_LO + PE_HALF:PE_LO + MLA_ROPE].set(inv_freq)
    sgn = jnp.zeros((1, LANES), F32).at[0, PE_LO:PE_LO + PE_HALF].set(-1.0)
    sgn = sgn.at[0, PE_LO + PE_HALF:PE_LO + MLA_ROPE].set(1.0)
    par = jnp.zeros((8, RWKV_WIDTH), F32)
    par = par.at[0].set(k_k).at[1].set(k_a).at[2].set(r_k.reshape(-1)).at[3].set(lnx_g).at[4].set(lnx_b)
    ln1 = jnp.stack([ln1_g, ln1_b])
    ln2 = jnp.stack([ln2_g, ln2_b])
    wo_r = w_out[:RWKV_WIDTH].astype(BF16)
    wo_m = w_out[RWKV_WIDTH:].astype(BF16)
    w1 = w_ffn1.astype(BF16)
    w2 = w_ffn2.astype(BF16)

    pos3 = positions.reshape(B, S, 1)

    tok = lambda: pl.BlockSpec((None, tm, RWKV_WIDTH), lambda b, j: (b, j, 0))
    tok_shape = jax.ShapeDtypeStruct((B, S, RWKV_WIDTH), F32)
    head_spec = pl.BlockSpec((None, H, tm, LANES), lambda b, j: (b, 0, j, 0))
    r, k, v, lw, a, g, q, kh, vt = pl.pallas_call(
        functools.partial(_prep_kernel, scale=float(dq) ** -0.5),
        grid=(B, nt),
        in_specs=[
            pl.BlockSpec((None, tm, D), lambda b, j: (b, j, 0)),
            pl.BlockSpec((None, tm, 1), lambda b, j: (b, j, 0)),
            _const_spec(w_all.shape), _const_spec(mu.shape), _const_spec(lora.shape),
            _const_spec(w0a0.shape), _const_spec(g2.shape),
            _const_spec((1, Q_LORA)), _const_spec(wq.shape),
            _const_spec((1, KV_LORA)), _const_spec(wk.shape), _const_spec(wvt.shape),
            _const_spec(freq.shape), _const_spec(sgn.shape),
        ],
        out_specs=[tok(), tok(), tok(), tok(), tok(), tok(), head_spec, head_spec,
                   pl.BlockSpec((None, None, H * MLA_V, tm), lambda b, j: (b, j, 0, 0))],
        out_shape=[tok_shape] * 6 + [
            jax.ShapeDtypeStruct((B, H, S, LANES), BF16),
            jax.ShapeDtypeStruct((B, H, S, LANES), BF16),
            jax.ShapeDtypeStruct((B, nt, H * MLA_V, tm), BF16)],
        scratch_shapes=[pltpu.VMEM((8, RWKV_COLS), F32)],
        compiler_params=pltpu.CompilerParams(
            dimension_semantics=("arbitrary", "arbitrary"), vmem_limit_bytes=VMEM_LIMIT),
    )(x, pos3, w_all, mu, lora, w0a0, g2, q_norm_g.reshape(1, Q_LORA), wq,
      kv_norm_g.reshape(1, KV_LORA), wk, wvt, freq, sgn)

    npair = RWKV_WIDTH // LANES
    pair = lambda: pl.BlockSpec((None, tb, LANES), lambda b, p, j: (b, j, p))
    y_rwkv = pl.pallas_call(
        functools.partial(_rwkv_kernel, chunk=chunk),
        grid=(B, npair, S // tb),
        in_specs=[pair(), pair(), pair(), pair(), pair(), pair(),
                  pl.BlockSpec((8, LANES), lambda b, p, j: (0, p))],
        out_specs=pair(),
        out_shape=jax.ShapeDtypeStruct((B, S, RWKV_WIDTH), BF16),
        scratch_shapes=[pltpu.VMEM((LANES, LANES), F32)],
        compiler_params=pltpu.CompilerParams(
            dimension_semantics=("arbitrary", "arbitrary", "arbitrary"),
            vmem_limit_bytes=VMEM_LIMIT),
    )(r, k, v, lw, a, g, par)

    y_mla_t = pl.pallas_call(
        _attn_kernel,
        grid=(B, H, S // tq),
        in_specs=[
            pl.BlockSpec((None, None, tq, LANES), lambda b, h, i: (b, h, i, 0)),
            pl.BlockSpec((None, None, S, LANES), lambda b, h, i: (b, h, 0, 0)),
            pl.BlockSpec((None, nt, MLA_V, tm), lambda b, h, i: (b, 0, h, 0)),
        ],
        out_specs=pl.BlockSpec((None, MLA_V, tq), lambda b, h, i: (b, h, i)),
        out_shape=jax.ShapeDtypeStruct((B, H * MLA_V, S), BF16),
        compiler_params=pltpu.CompilerParams(
            dimension_semantics=("arbitrary", "arbitrary", "arbitrary"),
            vmem_limit_bytes=VMEM_LIMIT),
    )(q, kh, vt)

    out = pl.pallas_call(
        functools.partial(_out_kernel, alpha=alpha, ff_chunk=_pick(D_FF, 1024)),
        grid=(B, nt),
        in_specs=[
            pl.BlockSpec((None, tm, D), lambda b, j: (b, j, 0)),
            pl.BlockSpec((None, tm, RWKV_WIDTH), lambda b, j: (b, j, 0)),
            pl.BlockSpec((None, MLA_WIDTH, tm), lambda b, j: (b, 0, j)),
            _const_spec(wo_r.shape), _const_spec(wo_m.shape), _const_spec(ln1.shape),
            _const_spec(w1.shape), _const_spec(w2.shape), _const_spec(ln2.shape),
        ],
        out_specs=pl.BlockSpec((None, tm, D), lambda b, j: (b, j, 0)),
        out_shape=jax.ShapeDtypeStruct((B, S, D), F32),
        compiler_params=pltpu.CompilerParams(
            dimension_semantics=("arbitrary", "arbitrary"), vmem_limit_bytes=VMEM_LIMIT),
    )(x, y_rwkv, y_mla_t, wo_r, wo_m, ln1, w1, w2, ln2)
    return out


def kernel(x, positions, w_in, shift_mu, decay_w0, decay_w2, iclr_a0, iclr_a2, gate_g2, k_k, k_a, r_k, lnx_g, lnx_b, q_norm_g, w_uq, kv_norm_g, w_ukv, w_out, ln1_g, ln1_b, w_ffn1, w_ffn2, ln2_g, ln2_b):
    depth = w_in.shape[0]
    alpha = (2 * depth) ** 0.25
    params = (w_in, shift_mu, decay_w0, decay_w2, iclr_a0, iclr_a2, gate_g2, k_k, k_a, r_k, lnx_g,
              lnx_b, q_norm_g, w_uq, kv_norm_g, w_ukv, w_out, ln1_g, ln1_b, w_ffn1, w_ffn2,
              ln2_g, ln2_b)
    for l in range(depth):
        x = _layer(x, positions, *(p[l] for p in params), alpha)
    return x
```

```python
import functools
import math

import jax
import jax.numpy as jnp
from jax import lax
from jax.experimental import pallas as pl
from jax.experimental.pallas import tpu as pltpu

F32 = jnp.float32
BF16 = jnp.bfloat16

D_MODEL = 1024
RWKV_HEAD = 64
RWKV_HEADS = 8
RWKV_WIDTH = RWKV_HEADS * RWKV_HEAD
DECAY_LORA = 64
ICLR_LORA = 64
GATE_LORA = 128
MLA_HEADS = 8
MLA_NOPE = 64
MLA_ROPE = 32
MLA_V = 64
MLA_WIDTH = MLA_HEADS * MLA_V
Q_LORA = 384
KV_LORA = 256
ROPE_BASE = 10000.0
D_FF = 4 * D_MODEL
LN_EPS = 1e-5
RMS_EPS = 1e-6
GN_EPS = 64e-5
RWKV_COLS = 3 * RWKV_WIDTH + DECAY_LORA + ICLR_LORA + GATE_LORA
MLA_COLS = Q_LORA + KV_LORA + MLA_ROPE

LANES = 128
VMEM_LIMIT = 56 * 1024 * 1024

C_R, C_K, C_V = 0, RWKV_WIDTH, 2 * RWKV_WIDTH
C_WA = 3 * RWKV_WIDTH
C_G = C_WA + DECAY_LORA + ICLR_LORA
C_Q = RWKV_COLS
C_KV = C_Q + Q_LORA
C_PE = C_KV + KV_LORA
IN_COLS_PAD = C_PE + LANES
PE_LO = MLA_NOPE
PE_HALF = MLA_ROPE // 2

CHUNK = 64
ATTN_GROUP = 2


def _dot(a, b):
    return jnp.dot(a.astype(BF16), b.astype(BF16), preferred_element_type=F32)


def _split_bf16(x):
    hi = x.astype(BF16)
    lo = (x - hi.astype(F32)).astype(BF16)
    return hi, lo


def _dot_left_split(x, w_exact):
    hi, lo = _split_bf16(x)
    return (jnp.dot(hi, w_exact, preferred_element_type=F32)
            + jnp.dot(lo, w_exact, preferred_element_type=F32))


def _sigmoid(z):
    return 1.0 / (1.0 + jnp.exp(-z))


def _layer_norm(x, g, b):
    mu = jnp.mean(x, axis=-1, keepdims=True)
    d = x - mu
    var = jnp.mean(d * d, axis=-1, keepdims=True)
    return d * lax.rsqrt(var + LN_EPS) * g + b


def _prep_kernel(x_ref, pos_ref, w_ref, mu_ref, lora_ref, w0a0_ref, g2_ref, qg_ref, wq_ref,
                 kvg_ref, wk_ref, wvt_ref, freq_ref, sgn_ref,
                 r_ref, k_ref, v_ref, lw_ref, a_ref, g_ref, q_ref, kh_ref, vt_ref, carry_ref,
                 *, scale):
    tm = x_ref.shape[0]
    j = pl.program_id(1)

    @pl.when(j == 0)
    def _():
        carry_ref[...] = jnp.zeros_like(carry_ref)

    h = jnp.dot(x_ref[...].astype(BF16), w_ref[...], preferred_element_type=F32)

    hr = h[:, :RWKV_COLS]
    row = lax.broadcasted_iota(jnp.int32, hr.shape, 0)
    prev = jnp.where(row == 0, carry_ref[0:1, :], pltpu.roll(hr, 1, axis=0))
    carry_ref[0:1, :] = hr[tm - 1:tm, :]
    hr = hr + (prev - hr) * mu_ref[...]

    r_ref[...] = hr[:, C_R:C_R + RWKV_WIDTH]
    k_ref[...] = hr[:, C_K:C_K + RWKV_WIDTH]
    v_ref[...] = hr[:, C_V:C_V + RWKV_WIDTH]

    lane = lax.broadcasted_iota(jnp.int32, (tm, LANES), 1)
    xwa = hr[:, C_WA:C_WA + LANES]
    xwa = jnp.where(lane < DECAY_LORA, jnp.tanh(xwa), xwa)
    z = _dot(xwa, lora_ref[...]) + w0a0_ref[...]
    sg = _sigmoid(z)
    lw_ref[...] = (-math.exp(-0.5)) * sg[:, :RWKV_WIDTH]
    a_ref[...] = sg[:, RWKV_WIDTH:]
    g_ref[...] = _dot(_sigmoid(hr[:, C_G:C_G + GATE_LORA]), g2_ref[...])

    def rms(c, gain):
        return c * lax.rsqrt(jnp.mean(c * c, axis=-1, keepdims=True) + RMS_EPS) * gain

    cqn = rms(h[:, C_Q:C_Q + Q_LORA], qg_ref[...]).astype(BF16)
    ckvn = rms(h[:, C_KV:C_KV + KV_LORA], kvg_ref[...]).astype(BF16)
    q_all = jnp.dot(cqn, wq_ref[...], preferred_element_type=F32)
    k_all = jnp.dot(ckvn, wk_ref[...], preferred_element_type=F32)
    vt = lax.dot_general(wvt_ref[...], ckvn, (((1,), (1,)), ((), ())),
                         preferred_element_type=F32)
    vt_ref[...] = vt.astype(vt_ref.dtype)

    ang = pos_ref[...].astype(F32) * freq_ref[...]
    cosf = jnp.cos(ang)
    sins = jnp.sin(ang) * sgn_ref[...]
    first_half = (lane >= PE_LO) & (lane < PE_LO + PE_HALF)

    def rope(zg):
        rot = jnp.where(first_half, pltpu.roll(zg, LANES - PE_HALF, axis=1),
                        pltpu.roll(zg, PE_HALF, axis=1))
        return zg * cosf + rot * sins

    kpe = rope(h[:, C_PE:C_PE + LANES])
    for hd in range(MLA_HEADS):
        sl = slice(hd * LANES, (hd + 1) * LANES)
        q_ref[hd] = (rope(q_all[:, sl]) * scale).astype(q_ref.dtype)
        kh_ref[hd] = (k_all[:, sl] + kpe).astype(kh_ref.dtype)


_NN = (((2,), (1,)), ((0,), (0,)))
_NT = (((2,), (2,)), ((0,), (0,)))
_TN = (((1,), (1,)), ((0,), (0,)))


def _rwkv_kernel(r_ref, k_ref, v_ref, lw_ref, a_ref, g_ref, par_ref, o_ref, h_ref, *, chunk):
    tb = r_ref.shape[0]
    C = chunk
    nc = tb // C
    C2 = 2 * C
    j = pl.program_id(2)

    @pl.when(j == 0)
    def _():
        h_ref[...] = jnp.zeros_like(h_ref)

    lane = lax.broadcasted_iota(jnp.int32, (1, LANES), 1)
    m0 = (lane < RWKV_HEAD).astype(F32)
    m1 = 1.0 - m0
    rr = lax.broadcasted_iota(jnp.int32, (LANES, LANES), 0)
    cc = lax.broadcasted_iota(jnp.int32, (LANES, LANES), 1)
    ones2 = ((rr < RWKV_HEAD) == (cc < RWKV_HEAD)).astype(BF16)
    eye = (rr == cc).astype(F32)

    k_k, k_a, r_k = par_ref[0:1, :], par_ref[1:2, :], par_ref[2:3, :]
    ln_g, ln_b = par_ref[3:4, :], par_ref[4:5, :]

    r = r_ref[...]
    k = k_ref[...]
    v = v_ref[...]
    lw = lw_ref[...]
    a = a_ref[...]

    kk0 = k * k_k
    n2 = _dot_left_split(kk0 * kk0, ones2)
    kk = kk0 / jnp.maximum(jnp.sqrt(n2), 1e-12)
    kmod = k * (1.0 + (a - 1.0) * k_a)
    bonus = _dot_left_split(r * kmod * r_k, ones2) * v
    bb = kk * a

    to3 = lambda t: t.reshape(nc, C, LANES)
    r3, k3, v3, lw3, kk3, bb3 = to3(r), to3(kmod), to3(v), to3(lw), to3(kk), to3(bb)

    ti = lax.broadcasted_iota(jnp.int32, (nc, C, C), 1)
    si = lax.broadcasted_iota(jnp.int32, (nc, C, C), 2)
    tril = (si <= ti).astype(BF16)
    lhi, llo = _split_bf16(lw3)
    cum = (lax.dot_general(tril, lhi, _NN, preferred_element_type=F32)
           + lax.dot_general(tril, llo, _NN, preferred_element_type=F32))
    cend = cum[:, C - 1:C, :]
    p_inc = jnp.exp(cum)
    p_exc = jnp.exp(cum - lw3)
    p_inv = jnp.exp(-cum)
    p_end = jnp.exp(cend - cum)
    p_all = jnp.exp(cend)

    m0b, m1b = m0.astype(BF16), m1.astype(BF16)

    def expand(t):
        tb = t.astype(BF16)
        return jnp.concatenate([tb * m0b, tb * m1b], axis=1)

    rt = r3 * p_inc
    at2 = expand(-kk3 * p_exc)
    rt2 = expand(rt)
    bt2 = expand(bb3 * p_inv)
    kt2 = expand(k3 * p_inv)
    v2 = expand(v3)
    bbar2 = expand(bb3 * p_end)
    kbar2 = expand(k3 * p_end)

    bdot = functools.partial(lax.dot_general, preferred_element_type=F32)
    lhs = jnp.concatenate([at2, rt2], axis=1)
    rhs = jnp.concatenate([bt2, kt2], axis=1)
    m_all = bdot(lhs, rhs, _NT)
    ri = lax.broadcasted_iota(jnp.int32, (4 * C, 4 * C), 0)
    ci = lax.broadcasted_iota(jnp.int32, (4 * C, 4 * C), 1)
    bottom = (ri >= C2).astype(jnp.int32)
    right = (ci >= C2).astype(jnp.int32)
    keep = (ri - C2 * bottom) - (ci - C2 * right) + bottom > 0
    m_all = jnp.where(keep[None], m_all, 0.0)
    a_ab = m_all[:, :C2, :C2]
    a_ak = m_all[:, :C2, C2:].astype(BF16)
    a_r = m_all[:, C2:, :].astype(BF16)

    pw = a_ab.astype(BF16)
    tmat = eye[None] + a_ab
    for _ in range(max(1, (C - 1).bit_length()) - 1):
        pw = bdot(pw, pw, _NN).astype(BF16)
        tmat = tmat + bdot(pw, tmat.astype(BF16), _NN)

    akv = bdot(a_ak, v2, _NN).astype(BF16)
    wu = bdot(tmat.astype(BF16), jnp.concatenate([at2, akv], axis=2), _NN)
    wu = wu.astype(BF16)
    zmat = jnp.concatenate(
        [wu, jnp.concatenate([jnp.zeros_like(v2), v2], axis=2)], axis=1)
    ry = bdot(a_r, zmat, _NN)
    rq2 = (jnp.concatenate([rt * m0, rt * m1], axis=1) + ry[:, :, :LANES]).astype(BF16)
    yadd2 = ry[:, :, LANES:]
    bk = jnp.concatenate([bbar2, kbar2], axis=1)
    gh = bdot(bk, zmat, _TN)
    g2 = (gh[:, :, :LANES] + eye[None] * p_all).astype(BF16)
    hadd2 = gh[:, :, LANES:]

    hst = h_ref[...]
    ys = []
    for c in range(nc):
        hb = hst.astype(BF16)
        y2 = jnp.dot(rq2[c], hb, preferred_element_type=F32) + yadd2[c]
        ys.append(y2[:C] + y2[C:])
        hst = jnp.dot(g2[c], hb, preferred_element_type=F32) + hadd2[c]
    h_ref[...] = hst
    y = jnp.concatenate(ys, axis=0)

    inv_n = 1.0 / RWKV_HEAD
    mu = _dot_left_split(y, ones2) * inv_n
    d = y - mu
    var = _dot_left_split(d * d, ones2) * inv_n
    yn = d * lax.rsqrt(var + GN_EPS) * ln_g + ln_b
    o_ref[...] = ((yn + bonus) * g_ref[...]).astype(o_ref.dtype)


def _attn_kernel(q_ref, k_ref, vt_ref, o_ref):
    nheads, tq, _ = q_ref.shape
    tk = vt_ref.shape[2]
    i = pl.program_id(2)
    dn = (((1,), (1,)), ((), ()))

    def block(jb, carry, masked):
        out = []
        for hd in range(nheads):
            m, l, acc = carry[hd]
            kb = k_ref[hd, pl.ds(pl.multiple_of(jb * tk, tk), tk), :]
            st = lax.dot_general(kb, q_ref[hd], dn, preferred_element_type=F32)
            if masked:
                kpos = jb * tk + lax.broadcasted_iota(jnp.int32, st.shape, 0)
                qpos = i * tq + lax.broadcasted_iota(jnp.int32, st.shape, 1)
                st = jnp.where(kpos <= qpos, st, -jnp.inf)
            mn = jnp.maximum(m, jnp.max(st, axis=0, keepdims=True))
            alpha = jnp.exp(m - mn)
            p = jnp.exp(st - mn)
            l = l * alpha + jnp.sum(p, axis=0, keepdims=True)
            vb = vt_ref[jb, hd * MLA_V:(hd + 1) * MLA_V, :]
            acc = acc * alpha + jnp.dot(vb, p.astype(BF16), preferred_element_type=F32)
            out.append((mn, l, acc))
        return tuple(out)

    init = tuple((jnp.full((1, tq), -jnp.inf, F32), jnp.zeros((1, tq), F32),
                  jnp.zeros((MLA_V, tq), F32)) for _ in range(nheads))
    nfull = (i * tq) // tk
    carry = lax.fori_loop(0, nfull, lambda jb, c: block(jb, c, False), init)
    for d in range(tq // tk):
        carry = block(nfull + d, carry, True)
    for hd in range(nheads):
        _, l, acc = carry[hd]
        o_ref[hd * MLA_V:(hd + 1) * MLA_V, :] = (acc / l).astype(o_ref.dtype)


def _out_kernel(x_ref, yr_ref, ymt_ref, wor_ref, wom_ref, ln1_ref, w1_ref, w2_ref, ln2_ref, o_ref,
                *, alpha, ff_chunk):
    x = x_ref[...]
    mix = jnp.dot(yr_ref[...], wor_ref[...], preferred_element_type=F32)
    mix = mix + lax.dot_general(ymt_ref[...], wom_ref[...], (((0,), (0,)), ((), ())),
                                preferred_element_type=F32)
    x1 = _layer_norm(alpha * x + mix, ln1_ref[0:1, :], ln1_ref[1:2, :])
    x1b = x1.astype(BF16)
    f = jnp.zeros_like(x1)
    for c in range(0, w1_ref.shape[1], ff_chunk):
        hid = jnp.dot(x1b, w1_ref[:, c:c + ff_chunk], preferred_element_type=F32)
        hid = jnp.square(jnp.maximum(hid, 0.0)).astype(BF16)
        f = f + jnp.dot(hid, w2_ref[c:c + ff_chunk, :], preferred_element_type=F32)
    o_ref[...] = _layer_norm(alpha * x1 + f, ln2_ref[0:1, :], ln2_ref[1:2, :])


def _const_spec(shape):
    nd = len(shape)
    return pl.BlockSpec(shape, lambda *_: (0,) * nd, pipeline_mode=pl.Buffered(1))


def _pick(n, pref):
    t = min(pref, n)
    while n % t:
        t //= 2
    return t


def _layer(x, positions, w_in, shift_mu, decay_w0, decay_w2, iclr_a0, iclr_a2, gate_g2, k_k, k_a,
           r_k, lnx_g, lnx_b, q_norm_g, w_uq, kv_norm_g, w_ukv, w_out, ln1_g, ln1_b, w_ffn1,
           w_ffn2, ln2_g, ln2_b, alpha):
    B, S, D = x.shape
    T = B * S
    H = MLA_HEADS
    tm = _pick(S, 512)
    tq = _pick(S, 512)
    tb = _pick(S, 512)
    chunk = _pick(tb, CHUNK)
    nt = S // tm

    w_r = w_in[:, :RWKV_COLS]
    w_m = w_in[:, RWKV_COLS:]
    w_pe = jnp.zeros((D, LANES), F32).at[:, PE_LO:PE_LO + MLA_ROPE].set(w_m[:, Q_LORA + KV_LORA:])
    w_all = jnp.concatenate([w_r, w_m[:, :Q_LORA + KV_LORA], w_pe], axis=1).astype(BF16)
    mu = shift_mu.reshape(1, RWKV_COLS)
    lora = jnp.zeros((LANES, 2 * RWKV_WIDTH), F32)
    lora = lora.at[:DECAY_LORA, :RWKV_WIDTH].set(decay_w2).at[DECAY_LORA:, RWKV_WIDTH:].set(iclr_a2)
    lora = lora.astype(BF16)
    w0a0 = jnp.concatenate([decay_w0, iclr_a0]).reshape(1, 2 * RWKV_WIDTH)
    g2 = gate_g2.astype(BF16)
    dq = MLA_NOPE + MLA_ROPE
    wq = jnp.zeros((Q_LORA, H, LANES), F32).at[:, :, :dq].set(w_uq.reshape(Q_LORA, H, dq))
    wq = wq.reshape(Q_LORA, H * LANES).astype(BF16)
    wkv = w_ukv.reshape(KV_LORA, H, MLA_NOPE + MLA_V)
    wk = jnp.zeros((KV_LORA, H, LANES), F32).at[:, :, :MLA_NOPE].set(wkv[:, :, :MLA_NOPE])
    wk = wk.reshape(KV_LORA, H * LANES).astype(BF16)
    wvt = wkv[:, :, MLA_NOPE:].reshape(KV_LORA, H * MLA_V).T.astype(BF16)
    inv_freq = ROPE_BASE ** (-jnp.arange(0, MLA_ROPE, 2, dtype=F32) / MLA_ROPE)
    freq = jnp.zeros((1, LANES), F32).at[0, PE_LO:PE_LO + PE_HALF].set(inv_freq)
    freq = freq.at[0, PE_LO + PE_HALF:PE_LO + MLA_ROPE].set(inv_freq)
    sgn = jnp.zeros((1, LANES), F32).at[0, PE_LO:PE_LO + PE_HALF].set(-1.0)
    sgn = sgn.at[0, PE_LO + PE_HALF:PE_LO + MLA_ROPE].set(1.0)
    par = jnp.zeros((8, RWKV_WIDTH), F32)
    par = par.at[0].set(k_k).at[1].set(k_a).at[2].set(r_k.reshape(-1)).at[3].set(lnx_g).at[4].set(lnx_b)
    ln1 = jnp.stack([ln1_g, ln1_b])
    ln2 = jnp.stack([ln2_g, ln2_b])
    wo_r = w_out[:RWKV_WIDTH].astype(BF16)
    wo_m = w_out[RWKV_WIDTH:].astype(BF16)
    w1 = w_ffn1.astype(BF16)
    w2 = w_ffn2.astype(BF16)

    pos3 = positions.reshape(B, S, 1)

    tok = lambda: pl.BlockSpec((None, tm, RWKV_WIDTH), lambda b, j: (b, j, 0))
    tok_shape = jax.ShapeDtypeStruct((B, S, RWKV_WIDTH), F32)
    head_spec = pl.BlockSpec((None, H, tm, LANES), lambda b, j: (b, 0, j, 0))
    r, k, v, lw, a, g, q, kh, vt = pl.pallas_call(
        functools.partial(_prep_kernel, scale=float(dq) ** -0.5),
        grid=(B, nt),
        in_specs=[
            pl.BlockSpec((None, tm, D), lambda b, j: (b, j, 0)),
            pl.BlockSpec((None, tm, 1), lambda b, j: (b, j, 0)),
            _const_spec(w_all.shape), _const_spec(mu.shape), _const_spec(lora.shape),
            _const_spec(w0a0.shape), _const_spec(g2.shape),
            _const_spec((1, Q_LORA)), _const_spec(wq.shape),
            _const_spec((1, KV_LORA)), _const_spec(wk.shape), _const_spec(wvt.shape),
            _const_spec(freq.shape), _const_spec(sgn.shape),
        ],
        out_specs=[tok(), tok(), tok(), tok(), tok(), tok(), head_spec, head_spec,
                   pl.BlockSpec((None, None, H * MLA_V, tm), lambda b, j: (b, j, 0, 0))],
        out_shape=[tok_shape] * 6 + [
            jax.ShapeDtypeStruct((B, H, S, LANES), BF16),
            jax.ShapeDtypeStruct((B, H, S, LANES), BF16),
            jax.ShapeDtypeStruct((B, nt, H * MLA_V, tm), BF16)],
        scratch_shapes=[pltpu.VMEM((8, RWKV_COLS), F32)],
        compiler_params=pltpu.CompilerParams(
            dimension_semantics=("arbitrary", "arbitrary"), vmem_limit_bytes=VMEM_LIMIT),
    )(x, pos3, w_all, mu, lora, w0a0, g2, q_norm_g.reshape(1, Q_LORA), wq,
      kv_norm_g.reshape(1, KV_LORA), wk, wvt, freq, sgn)

    npair = RWKV_WIDTH // LANES
    pair = lambda: pl.BlockSpec((None, tb, LANES), lambda b, p, j: (b, j, p))
    y_rwkv = pl.pallas_call(
        functools.partial(_rwkv_kernel, chunk=chunk),
        grid=(B, npair, S // tb),
        in_specs=[pair(), pair(), pair(), pair(), pair(), pair(),
                  pl.BlockSpec((8, LANES), lambda b, p, j: (0, p))],
        out_specs=pair(),
        out_shape=jax.ShapeDtypeStruct((B, S, RWKV_WIDTH), BF16),
        scratch_shapes=[pltpu.VMEM((LANES, LANES), F32)],
        compiler_params=pltpu.CompilerParams(
            dimension_semantics=("arbitrary", "arbitrary", "arbitrary"),
            vmem_limit_bytes=VMEM_LIMIT),
    )(r, k, v, lw, a, g, par)

    y_mla_t = pl.pallas_call(
        _attn_kernel,
        grid=(B, H // ATTN_GROUP, S // tq),
        in_specs=[
            pl.BlockSpec((None, ATTN_GROUP, tq, LANES), lambda b, h, i: (b, h, i, 0)),
            pl.BlockSpec((None, ATTN_GROUP, S, LANES), lambda b, h, i: (b, h, 0, 0)),
            pl.BlockSpec((None, nt, ATTN_GROUP * MLA_V, tm), lambda b, h, i: (b, 0, h, 0)),
        ],
        out_specs=pl.BlockSpec((None, ATTN_GROUP * MLA_V, tq), lambda b, h, i: (b, h, i)),
        out_shape=jax.ShapeDtypeStruct((B, H * MLA_V, S), BF16),
        compiler_params=pltpu.CompilerParams(
            dimension_semantics=("arbitrary", "arbitrary", "arbitrary"),
            vmem_limit_bytes=VMEM_LIMIT),
    )(q, kh, vt)

    out = pl.pallas_call(
        functools.partial(_out_kernel, alpha=alpha, ff_chunk=_pick(D_FF, 1024)),
        grid=(B, nt),
        in_specs=[
            pl.BlockSpec((None, tm, D), lambda b, j: (b, j, 0)),
            pl.BlockSpec((None, tm, RWKV_WIDTH), lambda b, j: (b, j, 0)),
            pl.BlockSpec((None, MLA_WIDTH, tm), lambda b, j: (b, 0, j)),
            _const_spec(wo_r.shape), _const_spec(wo_m.shape), _const_spec(ln1.shape),
            _const_spec(w1.shape), _const_spec(w2.shape), _const_spec(ln2.shape),
        ],
        out_specs=pl.BlockSpec((None, tm, D), lambda b, j: (b, j, 0)),
        out_shape=jax.ShapeDtypeStruct((B, S, D), F32),
        compiler_params=pltpu.CompilerParams(
            dimension_semantics=("arbitrary", "arbitrary"), vmem_limit_bytes=VMEM_LIMIT),
    )(x, y_rwkv, y_mla_t, wo_r, wo_m, ln1, w1, w2, ln2)
    return out


def kernel(x, positions, w_in, shift_mu, decay_w0, decay_w2, iclr_a0, iclr_a2, gate_g2, k_k, k_a, r_k, lnx_g, lnx_b, q_norm_g, w_uq, kv_norm_g, w_ukv, w_out, ln1_g, ln1_b, w_ffn1, w_ffn2, ln2_g, ln2_b):
    depth = w_in.shape[0]
    alpha = (2 * depth) ** 0.25
    params = (w_in, shift_mu, decay_w0, decay_w2, iclr_a0, iclr_a2, gate_g2, k_k, k_a, r_k, lnx_g,
              lnx_b, q_norm_g, w_uq, kv_norm_g, w_ukv, w_out, ln1_g, ln1_b, w_ffn1, w_ffn2,
              ln2_g, ln2_b)
    for l in range(depth):
        x = _layer(x, positions, *(p[l] for p in params), alpha)
    return x
```

```python
import functools
import math

import jax
import jax.numpy as jnp
from jax import lax
from jax.experimental import pallas as pl
from jax.experimental.pallas import tpu as pltpu

F32 = jnp.float32
BF16 = jnp.bfloat16

D_MODEL = 1024
RWKV_HEAD = 64
RWKV_HEADS = 8
RWKV_WIDTH = RWKV_HEADS * RWKV_HEAD
DECAY_LORA = 64
ICLR_LORA = 64
GATE_LORA = 128
MLA_HEADS = 8
MLA_NOPE = 64
MLA_ROPE = 32
MLA_V = 64
MLA_WIDTH = MLA_HEADS * MLA_V
Q_LORA = 384
KV_LORA = 256
ROPE_BASE = 10000.0
D_FF = 4 * D_MODEL
LN_EPS = 1e-5
RMS_EPS = 1e-6
GN_EPS = 64e-5
RWKV_COLS = 3 * RWKV_WIDTH + DECAY_LORA + ICLR_LORA + GATE_LORA
MLA_COLS = Q_LORA + KV_LORA + MLA_ROPE

LANES = 128
VMEM_LIMIT = 56 * 1024 * 1024

C_R, C_K, C_V = 0, RWKV_WIDTH, 2 * RWKV_WIDTH
C_WA = 3 * RWKV_WIDTH
C_G = C_WA + DECAY_LORA + ICLR_LORA
C_Q = RWKV_COLS
C_KV = C_Q + Q_LORA
C_PE = C_KV + KV_LORA
IN_COLS_PAD = C_PE + LANES
PE_LO = MLA_NOPE
PE_HALF = MLA_ROPE // 2

CHUNK = 64
ATTN_GROUP = 2
ATTN_ONES_ROWS = 16
ATTN_MAX_EXP2 = 92.0
ATTN_BOUND_SLACK = 1.01


def _dot(a, b):
    return jnp.dot(a.astype(BF16), b.astype(BF16), preferred_element_type=F32)


def _split_bf16(x):
    hi = x.astype(BF16)
    lo = (x - hi.astype(F32)).astype(BF16)
    return hi, lo


def _dot_left_split(x, w_exact):
    hi, lo = _split_bf16(x)
    return (jnp.dot(hi, w_exact, preferred_element_type=F32)
            + jnp.dot(lo, w_exact, preferred_element_type=F32))


def _sigmoid(z):
    return 1.0 / (1.0 + jnp.exp(-z))


def _layer_norm(x, g, b):
    mu = jnp.mean(x, axis=-1, keepdims=True)
    d = x - mu
    var = jnp.mean(d * d, axis=-1, keepdims=True)
    return d * lax.rsqrt(var + LN_EPS) * g + b


def _prep_kernel(x_ref, pos_ref, w_ref, mu_ref, lora_ref, w0a0_ref, g2_ref, qg_ref, wq_ref,
                 kvg_ref, wk_ref, wvt_ref, freq_ref, sgn_ref,
                 r_ref, k_ref, v_ref, lw_ref, a_ref, g_ref, q_ref, kh_ref, vt_ref, carry_ref,
                 *, scale):
    tm = x_ref.shape[0]
    j = pl.program_id(1)

    @pl.when(j == 0)
    def _():
        carry_ref[...] = jnp.zeros_like(carry_ref)

    h = jnp.dot(x_ref[...].astype(BF16), w_ref[...], preferred_element_type=F32)

    hr = h[:, :RWKV_COLS]
    row = lax.broadcasted_iota(jnp.int32, hr.shape, 0)
    prev = jnp.where(row == 0, carry_ref[0:1, :], pltpu.roll(hr, 1, axis=0))
    carry_ref[0:1, :] = hr[tm - 1:tm, :]
    hr = hr + (prev - hr) * mu_ref[...]

    r_ref[...] = hr[:, C_R:C_R + RWKV_WIDTH]
    k_ref[...] = hr[:, C_K:C_K + RWKV_WIDTH]
    v_ref[...] = hr[:, C_V:C_V + RWKV_WIDTH]

    lane = lax.broadcasted_iota(jnp.int32, (tm, LANES), 1)
    xwa = hr[:, C_WA:C_WA + LANES]
    xwa = jnp.where(lane < DECAY_LORA, jnp.tanh(xwa), xwa)
    z = _dot(xwa, lora_ref[...]) + w0a0_ref[...]
    sg = _sigmoid(z)
    lw_ref[...] = (-math.exp(-0.5)) * sg[:, :RWKV_WIDTH]
    a_ref[...] = sg[:, RWKV_WIDTH:]
    g_ref[...] = _dot(_sigmoid(hr[:, C_G:C_G + GATE_LORA]), g2_ref[...])

    def rms(c, gain):
        return c * lax.rsqrt(jnp.mean(c * c, axis=-1, keepdims=True) + RMS_EPS) * gain

    cqn = rms(h[:, C_Q:C_Q + Q_LORA], qg_ref[...]).astype(BF16)
    ckvn = rms(h[:, C_KV:C_KV + KV_LORA], kvg_ref[...]).astype(BF16)
    q_all = jnp.dot(cqn, wq_ref[...], preferred_element_type=F32)
    k_all = jnp.dot(ckvn, wk_ref[...], preferred_element_type=F32)
    vt = lax.dot_general(wvt_ref[...], ckvn, (((1,), (1,)), ((), ())),
                         preferred_element_type=F32)
    vt_ref[...] = vt.astype(vt_ref.dtype)

    ang = pos_ref[...].astype(F32) * freq_ref[...]
    cosf = jnp.cos(ang)
    sins = jnp.sin(ang) * sgn_ref[...]
    first_half = (lane >= PE_LO) & (lane < PE_LO + PE_HALF)

    def rope(zg):
        rot = jnp.where(first_half, pltpu.roll(zg, LANES - PE_HALF, axis=1),
                        pltpu.roll(zg, PE_HALF, axis=1))
        return zg * cosf + rot * sins

    kpe = rope(h[:, C_PE:C_PE + LANES])
    for hd in range(MLA_HEADS):
        sl = slice(hd * LANES, (hd + 1) * LANES)
        q_ref[hd] = (rope(q_all[:, sl]) * scale).astype(q_ref.dtype)
        kh_ref[hd] = (k_all[:, sl] + kpe).astype(kh_ref.dtype)


_NN = (((2,), (1,)), ((0,), (0,)))
_NT = (((2,), (2,)), ((0,), (0,)))
_TN = (((1,), (1,)), ((0,), (0,)))


def _rwkv_kernel(r_ref, k_ref, v_ref, lw_ref, a_ref, g_ref, par_ref, o_ref, h_ref, *, chunk):
    tb = r_ref.shape[0]
    C = chunk
    nc = tb // C
    C2 = 2 * C
    j = pl.program_id(2)

    @pl.when(j == 0)
    def _():
        h_ref[...] = jnp.zeros_like(h_ref)

    lane = lax.broadcasted_iota(jnp.int32, (1, LANES), 1)
    m0 = (lane < RWKV_HEAD).astype(F32)
    m1 = 1.0 - m0
    rr = lax.broadcasted_iota(jnp.int32, (LANES, LANES), 0)
    cc = lax.broadcasted_iota(jnp.int32, (LANES, LANES), 1)
    ones2 = ((rr < RWKV_HEAD) == (cc < RWKV_HEAD)).astype(BF16)
    eye = (rr == cc).astype(F32)

    k_k, k_a, r_k = par_ref[0:1, :], par_ref[1:2, :], par_ref[2:3, :]
    ln_g, ln_b = par_ref[3:4, :], par_ref[4:5, :]

    r = r_ref[...]
    k = k_ref[...]
    v = v_ref[...]
    lw = lw_ref[...]
    a = a_ref[...]

    kk0 = k * k_k
    n2 = _dot_left_split(kk0 * kk0, ones2)
    kk = kk0 / jnp.maximum(jnp.sqrt(n2), 1e-12)
    kmod = k * (1.0 + (a - 1.0) * k_a)
    bonus = _dot_left_split(r * kmod * r_k, ones2) * v
    bb = kk * a

    to3 = lambda t: t.reshape(nc, C, LANES)
    r3, k3, v3, lw3, kk3, bb3 = to3(r), to3(kmod), to3(v), to3(lw), to3(kk), to3(bb)

    ti = lax.broadcasted_iota(jnp.int32, (nc, C, C), 1)
    si = lax.broadcasted_iota(jnp.int32, (nc, C, C), 2)
    tril = (si <= ti).astype(BF16)
    lhi, llo = _split_bf16(lw3)
    cum = (lax.dot_general(tril, lhi, _NN, preferred_element_type=F32)
           + lax.dot_general(tril, llo, _NN, preferred_element_type=F32))
    cend = cum[:, C - 1:C, :]
    p_inc = jnp.exp(cum)
    p_exc = jnp.exp(cum - lw3)
    p_inv = jnp.exp(-cum)
    p_end = jnp.exp(cend - cum)
    p_all = jnp.exp(cend)

    m0b, m1b = m0.astype(BF16), m1.astype(BF16)

    def expand(t):
        tb = t.astype(BF16)
        return jnp.concatenate([tb * m0b, tb * m1b], axis=1)

    rt = r3 * p_inc
    at2 = expand(-kk3 * p_exc)
    rt2 = expand(rt)
    bt2 = expand(bb3 * p_inv)
    kt2 = expand(k3 * p_inv)
    v2 = expand(v3)
    bbar2 = expand(bb3 * p_end)
    kbar2 = expand(k3 * p_end)

    bdot = functools.partial(lax.dot_general, preferred_element_type=F32)
    lhs = jnp.concatenate([at2, rt2], axis=1)
    rhs = jnp.concatenate([bt2, kt2], axis=1)
    m_all = bdot(lhs, rhs, _NT)
    ri = lax.broadcasted_iota(jnp.int32, (4 * C, 4 * C), 0)
    ci = lax.broadcasted_iota(jnp.int32, (4 * C, 4 * C), 1)
    bottom = (ri >= C2).astype(jnp.int32)
    right = (ci >= C2).astype(jnp.int32)
    keep = (ri - C2 * bottom) - (ci - C2 * right) + bottom > 0
    m_all = jnp.where(keep[None], m_all, 0.0)
    a_ab = m_all[:, :C2, :C2]
    a_ak = m_all[:, :C2, C2:].astype(BF16)
    a_r = m_all[:, C2:, :].astype(BF16)

    pw = a_ab.astype(BF16)
    tmat = eye[None] + a_ab
    for _ in range(max(1, (C - 1).bit_length()) - 1):
        pw = bdot(pw, pw, _NN).astype(BF16)
        tmat = tmat + bdot(pw, tmat.astype(BF16), _NN)

    akv = bdot(a_ak, v2, _NN).astype(BF16)
    wu = bdot(tmat.astype(BF16), jnp.concatenate([at2, akv], axis=2), _NN)
    wu = wu.astype(BF16)
    zmat = jnp.concatenate(
        [wu, jnp.concatenate([jnp.zeros_like(v2), v2], axis=2)], axis=1)
    ry = bdot(a_r, zmat, _NN)
    rq2 = (jnp.concatenate([rt * m0, rt * m1], axis=1) + ry[:, :, :LANES]).astype(BF16)
    yadd2 = ry[:, :, LANES:]
    bk = jnp.concatenate([bbar2, kbar2], axis=1)
    gh = bdot(bk, zmat, _TN)
    g2 = (gh[:, :, :LANES] + eye[None] * p_all).astype(BF16)
    hadd2 = gh[:, :, LANES:]

    hst = h_ref[...]
    ys = []
    for c in range(nc):
        hb = hst.astype(BF16)
        y2 = jnp.dot(rq2[c], hb, preferred_element_type=F32) + yadd2[c]
        ys.append(y2[:C] + y2[C:])
        hst = jnp.dot(g2[c], hb, preferred_element_type=F32) + hadd2[c]
    h_ref[...] = hst
    y = jnp.concatenate(ys, axis=0)

    inv_n = 1.0 / RWKV_HEAD
    mu = _dot_left_split(y, ones2) * inv_n
    d = y - mu
    var = _dot_left_split(d * d, ones2) * inv_n
    yn = d * lax.rsqrt(var + GN_EPS) * ln_g + ln_b
    o_ref[...] = ((yn + bonus) * g_ref[...]).astype(o_ref.dtype)


def _attn_kernel(q_ref, k_ref, vt_ref, o_ref, kmax_ref, st_ref, acc_ref):
    nheads, t, _ = q_ref.shape
    seq = k_ref.shape[1]
    i = pl.program_id(2)
    dn = (((1,), (1,)), ((), ()))
    sub = 8

    @pl.when(i == 0)
    def _():
        for hd in range(nheads):
            ka = jnp.abs(k_ref[hd].astype(F32)).reshape(seq // sub, sub, LANES)
            kmax_ref[hd] = jnp.max(ka, axis=0)

    def scores(hd, jb):
        kb = k_ref[hd, pl.ds(pl.multiple_of(jb * t, t), t), :]
        return lax.dot_general(kb, q_ref[hd], dn, preferred_element_type=F32)

    def causal(x):
        krow = lax.broadcasted_iota(jnp.int32, x.shape, 0)
        qcol = lax.broadcasted_iota(jnp.int32, x.shape, 1)
        return jnp.where(krow <= qcol, x, -jnp.inf)

    refs, worst = [], []
    ones_rows = jnp.ones((sub, LANES), BF16)
    for hd in range(nheads):
        qh = q_ref[hd]
        k_own = k_ref[hd, pl.ds(pl.multiple_of(i * t, t), t), :]
        own = (qh.astype(F32) * k_own.astype(F32)).astype(BF16)
        ref = lax.dot_general(ones_rows, own, dn, preferred_element_type=F32)[0:1, :]
        bound = lax.dot_general(kmax_ref[hd].astype(BF16), jnp.abs(qh), dn,
                                preferred_element_type=F32)
        bound = jnp.max(bound, axis=0, keepdims=True) * ATTN_BOUND_SLACK
        refs.append(ref)
        worst.append(jnp.max(bound - ref))
    fast_ok = functools.reduce(jnp.maximum, worst) <= ATTN_MAX_EXP2

    @pl.when(fast_ok)
    def _():
        ones_v = jnp.ones((ATTN_ONES_ROWS, t), BF16)
        acc_ref[...] = jnp.zeros_like(acc_ref)

        def produce(jb, slot):
            for hd in range(nheads):
                st_ref[slot, hd] = scores(hd, jb)

        def consume(jb, slot, masked):
            for hd in range(nheads):
                x = st_ref[slot, hd] - refs[hd]
                p = jnp.exp2(causal(x) if masked else x).astype(BF16)
                vaug = jnp.concatenate(
                    [vt_ref[jb, hd * MLA_V:(hd + 1) * MLA_V, :], ones_v], axis=0)
                acc_ref[hd] += jnp.dot(vaug, p, preferred_element_type=F32)

        produce(0, 0)

        def pair(jj, c):
            j0 = 2 * jj
            produce(j0 + 1, 1)
            consume(j0, 0, False)
            produce(j0 + 2, 0)
            consume(j0 + 1, 1, False)
            return c

        lax.fori_loop(0, i // 2, pair, 0)

        @pl.when(i % 2 == 1)
        def _():
            produce(i, 1)
            consume(i - 1, 0, False)
            consume(i, 1, True)

        @pl.when(i % 2 == 0)
        def _():
            consume(i, 0, True)

        for hd in range(nheads):
            acc = acc_ref[hd]
            o_ref[hd * MLA_V:(hd + 1) * MLA_V, :] = (
                acc[:MLA_V] / acc[MLA_V:MLA_V + 1]).astype(o_ref.dtype)

    @pl.when(jnp.logical_not(fast_ok))
    def _():
        def block(jb, carry, masked):
            out = []
            for hd in range(nheads):
                m, l, acc = carry[hd]
                st = scores(hd, jb)
                if masked:
                    st = causal(st)
                mn = jnp.maximum(m, jnp.max(st, axis=0, keepdims=True))
                alpha = jnp.exp2(m - mn)
                p = jnp.exp2(st - mn)
                l = l * alpha + jnp.sum(p, axis=0, keepdims=True)
                vb = vt_ref[jb, hd * MLA_V:(hd + 1) * MLA_V, :]
                acc = acc * alpha + jnp.dot(vb, p.astype(BF16), preferred_element_type=F32)
                out.append((mn, l, acc))
            return tuple(out)

        init = tuple((jnp.full((1, t), -jnp.inf, F32), jnp.zeros((1, t), F32),
                      jnp.zeros((MLA_V, t), F32)) for _ in range(nheads))
        carry = lax.fori_loop(0, i, lambda jb, c: block(jb, c, False), init)
        carry = block(i, carry, True)
        for hd in range(nheads):
            _, l, acc = carry[hd]
            o_ref[hd * MLA_V:(hd + 1) * MLA_V, :] = (acc / l).astype(o_ref.dtype)


def _out_kernel(x_ref, yr_ref, ymt_ref, wor_ref, wom_ref, ln1_ref, w1_ref, w2_ref, ln2_ref, o_ref,
                *, alpha, ff_chunk):
    x = x_ref[...]
    mix = jnp.dot(yr_ref[...], wor_ref[...], preferred_element_type=F32)
    mix = mix + lax.dot_general(ymt_ref[...], wom_ref[...], (((0,), (0,)), ((), ())),
                                preferred_element_type=F32)
    x1 = _layer_norm(alpha * x + mix, ln1_ref[0:1, :], ln1_ref[1:2, :])
    x1b = x1.astype(BF16)
    f = jnp.zeros_like(x1)
    for c in range(0, w1_ref.shape[1], ff_chunk):
        hid = jnp.dot(x1b, w1_ref[:, c:c + ff_chunk], preferred_element_type=F32)
        hid = jnp.square(jnp.maximum(hid, 0.0)).astype(BF16)
        f = f + jnp.dot(hid, w2_ref[c:c + ff_chunk, :], preferred_element_type=F32)
    o_ref[...] = _layer_norm(alpha * x1 + f, ln2_ref[0:1, :], ln2_ref[1:2, :])


def _const_spec(shape):
    nd = len(shape)
    return pl.BlockSpec(shape, lambda *_: (0,) * nd, pipeline_mode=pl.Buffered(1))


def _pick(n, pref):
    t = min(pref, n)
    while n % t:
        t //= 2
    return t


def _layer(x, positions, w_in, shift_mu, decay_w0, decay_w2, iclr_a0, iclr_a2, gate_g2, k_k, k_a,
           r_k, lnx_g, lnx_b, q_norm_g, w_uq, kv_norm_g, w_ukv, w_out, ln1_g, ln1_b, w_ffn1,
           w_ffn2, ln2_g, ln2_b, alpha):
    B, S, D = x.shape
    T = B * S
    H = MLA_HEADS
    tm = _pick(S, 512)
    tb = _pick(S, 512)
    chunk = _pick(tb, CHUNK)
    nt = S // tm

    w_r = w_in[:, :RWKV_COLS]
    w_m = w_in[:, RWKV_COLS:]
    w_pe = jnp.zeros((D, LANES), F32).at[:, PE_LO:PE_LO + MLA_ROPE].set(w_m[:, Q_LORA + KV_LORA:])
    w_all = jnp.concatenate([w_r, w_m[:, :Q_LORA + KV_LORA], w_pe], axis=1).astype(BF16)
    mu = shift_mu.reshape(1, RWKV_COLS)
    lora = jnp.zeros((LANES, 2 * RWKV_WIDTH), F32)
    lora = lora.at[:DECAY_LORA, :RWKV_WIDTH].set(decay_w2).at[DECAY_LORA:, RWKV_WIDTH:].set(iclr_a2)
    lora = lora.astype(BF16)
    w0a0 = jnp.concatenate([decay_w0, iclr_a0]).reshape(1, 2 * RWKV_WIDTH)
    g2 = gate_g2.astype(BF16)
    dq = MLA_NOPE + MLA_ROPE
    wq = jnp.zeros((Q_LORA, H, LANES), F32).at[:, :, :dq].set(w_uq.reshape(Q_LORA, H, dq))
    wq = wq.reshape(Q_LORA, H * LANES).astype(BF16)
    wkv = w_ukv.reshape(KV_LORA, H, MLA_NOPE + MLA_V)
    wk = jnp.zeros((KV_LORA, H, LANES), F32).at[:, :, :MLA_NOPE].set(wkv[:, :, :MLA_NOPE])
    wk = wk.reshape(KV_LORA, H * LANES).astype(BF16)
    wvt = wkv[:, :, MLA_NOPE:].reshape(KV_LORA, H * MLA_V).T.astype(BF16)
    inv_freq = ROPE_BASE ** (-jnp.arange(0, MLA_ROPE, 2, dtype=F32) / MLA_ROPE)
    freq = jnp.zeros((1, LANES), F32).at[0, PE_LO:PE_LO + PE_HALF].set(inv_freq)
    freq = freq.at[0, PE_LO + PE_HALF:PE_LO + MLA_ROPE].set(inv_freq)
    sgn = jnp.zeros((1, LANES), F32).at[0, PE_LO:PE_LO + PE_HALF].set(-1.0)
    sgn = sgn.at[0, PE_LO + PE_HALF:PE_LO + MLA_ROPE].set(1.0)
    par = jnp.zeros((8, RWKV_WIDTH), F32)
    par = par.at[0].set(k_k).at[1].set(k_a).at[2].set(r_k.reshape(-1)).at[3].set(lnx_g).at[4].set(lnx_b)
    ln1 = jnp.stack([ln1_g, ln1_b])
    ln2 = jnp.stack([ln2_g, ln2_b])
    wo_r = w_out[:RWKV_WIDTH].astype(BF16)
    wo_m = w_out[RWKV_WIDTH:].astype(BF16)
    w1 = w_ffn1.astype(BF16)
    w2 = w_ffn2.astype(BF16)

    pos3 = positions.reshape(B, S, 1)

    tok = lambda: pl.BlockSpec((None, tm, RWKV_WIDTH), lambda b, j: (b, j, 0))
    tok_shape = jax.ShapeDtypeStruct((B, S, RWKV_WIDTH), F32)
    head_spec = pl.BlockSpec((None, H, tm, LANES), lambda b, j: (b, 0, j, 0))
    r, k, v, lw, a, g, q, kh, vt = pl.pallas_call(
        functools.partial(_prep_kernel, scale=float(dq) ** -0.5 * math.log2(math.e)),
        grid=(B, nt),
        in_specs=[
            pl.BlockSpec((None, tm, D), lambda b, j: (b, j, 0)),
            pl.BlockSpec((None, tm, 1), lambda b, j: (b, j, 0)),
            _const_spec(w_all.shape), _const_spec(mu.shape), _const_spec(lora.shape),
            _const_spec(w0a0.shape), _const_spec(g2.shape),
            _const_spec((1, Q_LORA)), _const_spec(wq.shape),
            _const_spec((1, KV_LORA)), _const_spec(wk.shape), _const_spec(wvt.shape),
            _const_spec(freq.shape), _const_spec(sgn.shape),
        ],
        out_specs=[tok(), tok(), tok(), tok(), tok(), tok(), head_spec, head_spec,
                   pl.BlockSpec((None, None, H * MLA_V, tm), lambda b, j: (b, j, 0, 0))],
        out_shape=[tok_shape] * 6 + [
            jax.ShapeDtypeStruct((B, H, S, LANES), BF16),
            jax.ShapeDtypeStruct((B, H, S, LANES), BF16),
            jax.ShapeDtypeStruct((B, nt, H * MLA_V, tm), BF16)],
        scratch_shapes=[pltpu.VMEM((8, RWKV_COLS), F32)],
        compiler_params=pltpu.CompilerParams(
            dimension_semantics=("arbitrary", "arbitrary"), vmem_limit_bytes=VMEM_LIMIT),
    )(x, pos3, w_all, mu, lora, w0a0, g2, q_norm_g.reshape(1, Q_LORA), wq,
      kv_norm_g.reshape(1, KV_LORA), wk, wvt, freq, sgn)

    npair = RWKV_WIDTH // LANES
    pair = lambda: pl.BlockSpec((None, tb, LANES), lambda b, p, j: (b, j, p))
    y_rwkv = pl.pallas_call(
        functools.partial(_rwkv_kernel, chunk=chunk),
        grid=(B, npair, S // tb),
        in_specs=[pair(), pair(), pair(), pair(), pair(), pair(),
                  pl.BlockSpec((8, LANES), lambda b, p, j: (0, p))],
        out_specs=pair(),
        out_shape=jax.ShapeDtypeStruct((B, S, RWKV_WIDTH), BF16),
        scratch_shapes=[pltpu.VMEM((LANES, LANES), F32)],
        compiler_params=pltpu.CompilerParams(
            dimension_semantics=("arbitrary", "arbitrary", "arbitrary"),
            vmem_limit_bytes=VMEM_LIMIT),
    )(r, k, v, lw, a, g, par)

    y_mla_t = pl.pallas_call(
        _attn_kernel,
        grid=(B, H // ATTN_GROUP, nt),
        in_specs=[
            pl.BlockSpec((None, ATTN_GROUP, tm, LANES), lambda b, h, i: (b, h, i, 0)),
            pl.BlockSpec((None, ATTN_GROUP, S, LANES), lambda b, h, i: (b, h, 0, 0)),
            pl.BlockSpec((None, nt, ATTN_GROUP * MLA_V, tm), lambda b, h, i: (b, 0, h, 0)),
        ],
        out_specs=pl.BlockSpec((None, ATTN_GROUP * MLA_V, tm), lambda b, h, i: (b, h, i)),
        out_shape=jax.ShapeDtypeStruct((B, H * MLA_V, S), BF16),
        scratch_shapes=[pltpu.VMEM((ATTN_GROUP, 8, LANES), F32),
                        pltpu.VMEM((2, ATTN_GROUP, tm, tm), F32),
                        pltpu.VMEM((ATTN_GROUP, MLA_V + ATTN_ONES_ROWS, tm), F32)],
        compiler_params=pltpu.CompilerParams(
            dimension_semantics=("arbitrary", "arbitrary", "arbitrary"),
            vmem_limit_bytes=VMEM_LIMIT),
    )(q, kh, vt)

    out = pl.pallas_call(
        functools.partial(_out_kernel, alpha=alpha, ff_chunk=_pick(D_FF, 1024)),
        grid=(B, nt),
        in_specs=[
            pl.BlockSpec((None, tm, D), lambda b, j: (b, j, 0)),
            pl.BlockSpec((None, tm, RWKV_WIDTH), lambda b, j: (b, j, 0)),
            pl.BlockSpec((None, MLA_WIDTH, tm), lambda b, j: (b, 0, j)),
            _const_spec(wo_r.shape), _const_spec(wo_m.shape), _const_spec(ln1.shape),
            _const_spec(w1.shape), _const_spec(w2.shape), _const_spec(ln2.shape),
        ],
        out_specs=pl.BlockSpec((None, tm, D), lambda b, j: (b, j, 0)),
        out_shape=jax.ShapeDtypeStruct((B, S, D), F32),
        compiler_params=pltpu.CompilerParams(
            dimension_semantics=("arbitrary", "arbitrary"), vmem_limit_bytes=VMEM_LIMIT),
    )(x, y_rwkv, y_mla_t, wo_r, wo_m, ln1, w1, w2, ln2)
    return out


def kernel(x, positions, w_in, shift_mu, decay_w0, decay_w2, iclr_a0, iclr_a2, gate_g2, k_k, k_a, r_k, lnx_g, lnx_b, q_norm_g, w_uq, kv_norm_g, w_ukv, w_out, ln1_g, ln1_b, w_ffn1, w_ffn2, ln2_g, ln2_b):
    depth = w_in.shape[0]
    alpha = (2 * depth) ** 0.25
    params = (w_in, shift_mu, decay_w0, decay_w2, iclr_a0, iclr_a2, gate_g2, k_k, k_a, r_k, lnx_g,
              lnx_b, q_norm_g, w_uq, kv_norm_g, w_ukv, w_out, ln1_g, ln1_b, w_ffn1, w_ffn2,
              ln2_g, ln2_b)
    for l in range(depth):
        x = _layer(x, positions, *(p[l] for p in params), alpha)
    return x
```

```python
import functools
import math

import jax
import jax.numpy as jnp
from jax import lax
from jax.experimental import pallas as pl
from jax.experimental.pallas import tpu as pltpu

F32 = jnp.float32
BF16 = jnp.bfloat16

D_MODEL = 1024
RWKV_HEAD = 64
RWKV_HEADS = 8
RWKV_WIDTH = RWKV_HEADS * RWKV_HEAD
DECAY_LORA = 64
ICLR_LORA = 64
GATE_LORA = 128
MLA_HEADS = 8
MLA_NOPE = 64
MLA_ROPE = 32
MLA_V = 64
MLA_WIDTH = MLA_HEADS * MLA_V
Q_LORA = 384
KV_LORA = 256
ROPE_BASE = 10000.0
D_FF = 4 * D_MODEL
LN_EPS = 1e-5
RMS_EPS = 1e-6
GN_EPS = 64e-5
RWKV_COLS = 3 * RWKV_WIDTH + DECAY_LORA + ICLR_LORA + GATE_LORA
MLA_COLS = Q_LORA + KV_LORA + MLA_ROPE

LANES = 128
VMEM_LIMIT = 56 * 1024 * 1024

C_R, C_K, C_V = 0, RWKV_WIDTH, 2 * RWKV_WIDTH
C_WA = 3 * RWKV_WIDTH
C_G = C_WA + DECAY_LORA + ICLR_LORA
C_Q = RWKV_COLS
C_KV = C_Q + Q_LORA
C_PE = C_KV + KV_LORA
IN_COLS_PAD = C_PE + LANES
PE_LO = MLA_NOPE
PE_HALF = MLA_ROPE // 2

CHUNK = 64
ATTN_GROUP = 2
ATTN_ONES_ROWS = 16
ATTN_MAX_EXP2 = 92.0
ATTN_BOUND_SLACK = 1.01


def _dot(a, b):
    return jnp.dot(a.astype(BF16), b.astype(BF16), preferred_element_type=F32)


def _split_bf16(x):
    hi = x.astype(BF16)
    lo = (x - hi.astype(F32)).astype(BF16)
    return hi, lo


def _dot_left_split(x, w_exact):
    hi, lo = _split_bf16(x)
    return (jnp.dot(hi, w_exact, preferred_element_type=F32)
            + jnp.dot(lo, w_exact, preferred_element_type=F32))


def _sigmoid(z):
    return 1.0 / (1.0 + jnp.exp(-z))


def _layer_norm(x, g, b):
    mu = jnp.mean(x, axis=-1, keepdims=True)
    d = x - mu
    var = jnp.mean(d * d, axis=-1, keepdims=True)
    return d * lax.rsqrt(var + LN_EPS) * g + b


def _prep_kernel(x_ref, pos_ref, w_ref, mu_ref, lora_ref, w0a0_ref, g2_ref, qg_ref, wq_ref,
                 kvg_ref, wk_ref, wvt_ref, freq_ref, sgn_ref,
                 r_ref, k_ref, v_ref, lw_ref, a_ref, g_ref, q_ref, kh_ref, vt_ref, carry_ref,
                 *, scale):
    tm = x_ref.shape[0]
    j = pl.program_id(1)

    @pl.when(j == 0)
    def _():
        carry_ref[...] = jnp.zeros_like(carry_ref)

    h = jnp.dot(x_ref[...].astype(BF16), w_ref[...], preferred_element_type=F32)

    hr = h[:, :RWKV_COLS]
    row = lax.broadcasted_iota(jnp.int32, hr.shape, 0)
    prev = jnp.where(row == 0, carry_ref[0:1, :], pltpu.roll(hr, 1, axis=0))
    carry_ref[0:1, :] = hr[tm - 1:tm, :]
    hr = hr + (prev - hr) * mu_ref[...]

    r_ref[...] = hr[:, C_R:C_R + RWKV_WIDTH]
    k_ref[...] = hr[:, C_K:C_K + RWKV_WIDTH]
    v_ref[...] = hr[:, C_V:C_V + RWKV_WIDTH]

    lane = lax.broadcasted_iota(jnp.int32, (tm, LANES), 1)
    xwa = hr[:, C_WA:C_WA + LANES]
    xwa = jnp.where(lane < DECAY_LORA, jnp.tanh(xwa), xwa)
    z = _dot(xwa, lora_ref[...]) + w0a0_ref[...]
    sg = _sigmoid(z)
    lw_ref[...] = (-math.exp(-0.5)) * sg[:, :RWKV_WIDTH]
    a_ref[...] = sg[:, RWKV_WIDTH:]
    g_ref[...] = _dot(_sigmoid(hr[:, C_G:C_G + GATE_LORA]), g2_ref[...])

    def rms(c, gain):
        return c * lax.rsqrt(jnp.mean(c * c, axis=-1, keepdims=True) + RMS_EPS) * gain

    cqn = rms(h[:, C_Q:C_Q + Q_LORA], qg_ref[...]).astype(BF16)
    ckvn = rms(h[:, C_KV:C_KV + KV_LORA], kvg_ref[...]).astype(BF16)
    q_all = jnp.dot(cqn, wq_ref[...], preferred_element_type=F32)
    k_all = jnp.dot(ckvn, wk_ref[...], preferred_element_type=F32)
    vt = lax.dot_general(wvt_ref[...], ckvn, (((1,), (1,)), ((), ())),
                         preferred_element_type=F32)
    vt_ref[...] = vt.astype(vt_ref.dtype)

    ang = pos_ref[...].astype(F32) * freq_ref[...]
    cosf = jnp.cos(ang)
    sins = jnp.sin(ang) * sgn_ref[...]
    first_half = (lane >= PE_LO) & (lane < PE_LO + PE_HALF)

    def rope(zg):
        rot = jnp.where(first_half, pltpu.roll(zg, LANES - PE_HALF, axis=1),
                        pltpu.roll(zg, PE_HALF, axis=1))
        return zg * cosf + rot * sins

    kpe = rope(h[:, C_PE:C_PE + LANES])
    for hd in range(MLA_HEADS):
        sl = slice(hd * LANES, (hd + 1) * LANES)
        q_ref[hd] = (rope(q_all[:, sl]) * scale).astype(q_ref.dtype)
        kh_ref[hd] = (k_all[:, sl] + kpe).astype(kh_ref.dtype)


_NN = (((2,), (1,)), ((0,), (0,)))
_NT = (((2,), (2,)), ((0,), (0,)))
_TN = (((1,), (1,)), ((0,), (0,)))


def _rwkv_kernel(r_ref, k_ref, v_ref, lw_ref, a_ref, gprev_ref, par_ref, parprev_ref, o_ref,
                 h_ref, rq_ref, ya_ref, g2_ref, ha_ref, bonus_ref, *, chunk, nsteps):
    tb = r_ref.shape[0]
    C = chunk
    nc = tb // C
    C2 = 2 * C
    s = pl.program_id(0)

    @pl.when(s == 0)
    def _():
        for ref in (h_ref, rq_ref, ya_ref, g2_ref, ha_ref, bonus_ref):
            ref[...] = jnp.zeros_like(ref)

    first_of_sequence = (jnp.maximum(s - 1, 0) % nsteps) == 0
    scan = {"h": h_ref[...] * jnp.where(first_of_sequence, 0.0, 1.0), "ys": []}

    def scan_step():
        c = len(scan["ys"])
        if c < nc:
            hb = scan["h"].astype(BF16)
            y2 = jnp.dot(rq_ref[c], hb, preferred_element_type=F32) + ya_ref[c]
            scan["ys"].append(y2[:C] + y2[C:])
            scan["h"] = jnp.dot(g2_ref[c], hb, preferred_element_type=F32) + ha_ref[c]

    lane = lax.broadcasted_iota(jnp.int32, (1, LANES), 1)
    m0 = (lane < RWKV_HEAD).astype(F32)
    m1 = 1.0 - m0
    rr = lax.broadcasted_iota(jnp.int32, (LANES, LANES), 0)
    cc = lax.broadcasted_iota(jnp.int32, (LANES, LANES), 1)
    ones2 = ((rr < RWKV_HEAD) == (cc < RWKV_HEAD)).astype(BF16)
    eye = (rr == cc).astype(F32)

    k_k, k_a, r_k = par_ref[0:1, :], par_ref[1:2, :], par_ref[2:3, :]
    ln_g, ln_b = parprev_ref[3:4, :], parprev_ref[4:5, :]

    r = r_ref[...]
    k = k_ref[...]
    v = v_ref[...]
    lw = lw_ref[...]
    a = a_ref[...]

    kk0 = k * k_k
    n2 = _dot_left_split(kk0 * kk0, ones2)
    kk = kk0 / jnp.maximum(jnp.sqrt(n2), 1e-12)
    kmod = k * (1.0 + (a - 1.0) * k_a)
    bonus = _dot_left_split(r * kmod * r_k, ones2) * v
    bb = kk * a

    to3 = lambda t: t.reshape(nc, C, LANES)
    r3, k3, v3, lw3, kk3, bb3 = to3(r), to3(kmod), to3(v), to3(lw), to3(kk), to3(bb)

    ti = lax.broadcasted_iota(jnp.int32, (nc, C, C), 1)
    si = lax.broadcasted_iota(jnp.int32, (nc, C, C), 2)
    tril = (si <= ti).astype(BF16)
    lhi, llo = _split_bf16(lw3)
    cum = (lax.dot_general(tril, lhi, _NN, preferred_element_type=F32)
           + lax.dot_general(tril, llo, _NN, preferred_element_type=F32))
    cend = cum[:, C - 1:C, :]
    p_inc = jnp.exp(cum)
    p_exc = jnp.exp(cum - lw3)
    p_inv = jnp.exp(-cum)
    p_end = jnp.exp(cend - cum)
    p_all = jnp.exp(cend)

    m0b, m1b = m0.astype(BF16), m1.astype(BF16)

    def expand(t):
        tb = t.astype(BF16)
        return jnp.concatenate([tb * m0b, tb * m1b], axis=1)

    rt = r3 * p_inc
    at2 = expand(-kk3 * p_exc)
    rt2 = expand(rt)
    bt2 = expand(bb3 * p_inv)
    kt2 = expand(k3 * p_inv)
    v2 = expand(v3)
    bbar2 = expand(bb3 * p_end)
    kbar2 = expand(k3 * p_end)

    bdot = functools.partial(lax.dot_general, preferred_element_type=F32)
    lhs = jnp.concatenate([at2, rt2], axis=1)
    rhs = jnp.concatenate([bt2, kt2], axis=1)
    m_all = bdot(lhs, rhs, _NT)
    ri = lax.broadcasted_iota(jnp.int32, (4 * C, 4 * C), 0)
    ci = lax.broadcasted_iota(jnp.int32, (4 * C, 4 * C), 1)
    bottom = (ri >= C2).astype(jnp.int32)
    right = (ci >= C2).astype(jnp.int32)
    keep = (ri - C2 * bottom) - (ci - C2 * right) + bottom > 0
    m_all = jnp.where(keep[None], m_all, 0.0)
    a_ab = m_all[:, :C2, :C2]
    a_ak = m_all[:, :C2, C2:].astype(BF16)
    a_r = m_all[:, C2:, :].astype(BF16)
    scan_step()

    rounds = max(1, (C - 1).bit_length()) - 1
    tmat = eye[None] + a_ab
    pw = a_ab.astype(BF16)
    pw = bdot(pw, pw, _NN).astype(BF16)
    scan_step()
    for rnd in range(rounds):
        if rnd + 1 < rounds:
            both = bdot(pw, jnp.concatenate([tmat.astype(BF16), pw], axis=2), _NN)
            tmat = tmat + both[:, :, :C2]
            pw = both[:, :, C2:].astype(BF16)
        else:
            tmat = tmat + bdot(pw, tmat.astype(BF16), _NN)
        scan_step()

    akv = bdot(a_ak, v2, _NN).astype(BF16)
    scan_step()
    wu = bdot(tmat.astype(BF16), jnp.concatenate([at2, akv], axis=2), _NN)
    wu = wu.astype(BF16)
    while len(scan["ys"]) < nc:
        scan_step()
    h_ref[...] = scan["h"]

    y = jnp.concatenate(scan["ys"], axis=0)
    inv_n = 1.0 / RWKV_HEAD
    mu = _dot_left_split(y, ones2) * inv_n
    d = y - mu
    var = _dot_left_split(d * d, ones2) * inv_n
    yn = d * lax.rsqrt(var + GN_EPS) * ln_g + ln_b
    o_ref[...] = ((yn + bonus_ref[...]) * gprev_ref[...]).astype(o_ref.dtype)

    zmat = jnp.concatenate(
        [wu, jnp.concatenate([jnp.zeros_like(v2), v2], axis=2)], axis=1)
    ry = bdot(a_r, zmat, _NN)
    bk = jnp.concatenate([bbar2, kbar2], axis=1)
    gh = bdot(bk, zmat, _TN)

    rq_ref[...] = (jnp.concatenate([rt * m0, rt * m1], axis=1) + ry[:, :, :LANES]).astype(BF16)
    ya_ref[...] = ry[:, :, LANES:]
    g2_ref[...] = (gh[:, :, :LANES] + eye[None] * p_all).astype(BF16)
    ha_ref[...] = gh[:, :, LANES:]
    bonus_ref[...] = bonus


def _attn_kernel(q_ref, k_ref, vt_ref, o_ref, kmax_ref, st_ref, acc_ref):
    nheads, t, _ = q_ref.shape
    seq = k_ref.shape[1]
    i = pl.program_id(2)
    dn = (((1,), (1,)), ((), ()))
    sub = 8

    @pl.when(i == 0)
    def _():
        for hd in range(nheads):
            ka = jnp.abs(k_ref[hd].astype(F32)).reshape(seq // sub, sub, LANES)
            kmax_ref[hd] = jnp.max(ka, axis=0)

    def scores(hd, jb):
        kb = k_ref[hd, pl.ds(pl.multiple_of(jb * t, t), t), :]
        return lax.dot_general(kb, q_ref[hd], dn, preferred_element_type=F32)

    def causal(x):
        krow = lax.broadcasted_iota(jnp.int32, x.shape, 0)
        qcol = lax.broadcasted_iota(jnp.int32, x.shape, 1)
        return jnp.where(krow <= qcol, x, -jnp.inf)

    refs, worst = [], []
    ones_rows = jnp.ones((sub, LANES), BF16)
    for hd in range(nheads):
        qh = q_ref[hd]
        k_own = k_ref[hd, pl.ds(pl.multiple_of(i * t, t), t), :]
        own = (qh.astype(F32) * k_own.astype(F32)).astype(BF16)
        ref = lax.dot_general(ones_rows, own, dn, preferred_element_type=F32)[0:1, :]
        bound = lax.dot_general(kmax_ref[hd].astype(BF16), jnp.abs(qh), dn,
                                preferred_element_type=F32)
        bound = jnp.max(bound, axis=0, keepdims=True) * ATTN_BOUND_SLACK
        refs.append(ref)
        worst.append(jnp.max(bound - ref))
    fast_ok = functools.reduce(jnp.maximum, worst) <= ATTN_MAX_EXP2

    @pl.when(fast_ok)
    def _():
        ones_v = jnp.ones((ATTN_ONES_ROWS, t), BF16)
        acc_ref[...] = jnp.zeros_like(acc_ref)

        def produce(jb, slot):
            for hd in range(nheads):
                st_ref[slot, hd] = scores(hd, jb)

        def consume(jb, slot, masked):
            for hd in range(nheads):
                x = st_ref[slot, hd] - refs[hd]
                p = jnp.exp2(causal(x) if masked else x).astype(BF16)
                vaug = jnp.concatenate(
                    [vt_ref[jb, hd * MLA_V:(hd + 1) * MLA_V, :], ones_v], axis=0)
                acc_ref[hd] += jnp.dot(vaug, p, preferred_element_type=F32)

        produce(0, 0)

        def pair(jj, c):
            j0 = 2 * jj
            produce(j0 + 1, 1)
            consume(j0, 0, False)
            produce(j0 + 2, 0)
            consume(j0 + 1, 1, False)
            return c

        lax.fori_loop(0, i // 2, pair, 0)

        @pl.when(i % 2 == 1)
        def _():
            produce(i, 1)
            consume(i - 1, 0, False)
            consume(i, 1, True)

        @pl.when(i % 2 == 0)
        def _():
            consume(i, 0, True)

        for hd in range(nheads):
            acc = acc_ref[hd]
            o_ref[hd * MLA_V:(hd + 1) * MLA_V, :] = (
                acc[:MLA_V] / acc[MLA_V:MLA_V + 1]).astype(o_ref.dtype)

    @pl.when(jnp.logical_not(fast_ok))
    def _():
        def block(jb, carry, masked):
            out = []
            for hd in range(nheads):
                m, l, acc = carry[hd]
                st = scores(hd, jb)
                if masked:
                    st = causal(st)
                mn = jnp.maximum(m, jnp.max(st, axis=0, keepdims=True))
                alpha = jnp.exp2(m - mn)
                p = jnp.exp2(st - mn)
                l = l * alpha + jnp.sum(p, axis=0, keepdims=True)
                vb = vt_ref[jb, hd * MLA_V:(hd + 1) * MLA_V, :]
                acc = acc * alpha + jnp.dot(vb, p.astype(BF16), preferred_element_type=F32)
                out.append((mn, l, acc))
            return tuple(out)

        init = tuple((jnp.full((1, t), -jnp.inf, F32), jnp.zeros((1, t), F32),
                      jnp.zeros((MLA_V, t), F32)) for _ in range(nheads))
        carry = lax.fori_loop(0, i, lambda jb, c: block(jb, c, False), init)
        carry = block(i, carry, True)
        for hd in range(nheads):
            _, l, acc = carry[hd]
            o_ref[hd * MLA_V:(hd + 1) * MLA_V, :] = (acc / l).astype(o_ref.dtype)


def _out_kernel(x_ref, yr_ref, ymt_ref, wor_ref, wom_ref, ln1_ref, w1_ref, w2_ref, ln2_ref, o_ref,
                *, alpha, ff_chunk):
    x = x_ref[...]
    mix = jnp.dot(yr_ref[...], wor_ref[...], preferred_element_type=F32)
    mix = mix + lax.dot_general(ymt_ref[...], wom_ref[...], (((0,), (0,)), ((), ())),
                                preferred_element_type=F32)
    x1 = _layer_norm(alpha * x + mix, ln1_ref[0:1, :], ln1_ref[1:2, :])
    x1b = x1.astype(BF16)
    f = jnp.zeros_like(x1)
    for c in range(0, w1_ref.shape[1], ff_chunk):
        hid = jnp.dot(x1b, w1_ref[:, c:c + ff_chunk], preferred_element_type=F32)
        hid = jnp.square(jnp.maximum(hid, 0.0)).astype(BF16)
        f = f + jnp.dot(hid, w2_ref[c:c + ff_chunk, :], preferred_element_type=F32)
    o_ref[...] = _layer_norm(alpha * x1 + f, ln2_ref[0:1, :], ln2_ref[1:2, :])


def _const_spec(shape):
    nd = len(shape)
    return pl.BlockSpec(shape, lambda *_: (0,) * nd, pipeline_mode=pl.Buffered(1))


def _pick(n, pref):
    t = min(pref, n)
    while n % t:
        t //= 2
    return t


def _layer(x, positions, w_in, shift_mu, decay_w0, decay_w2, iclr_a0, iclr_a2, gate_g2, k_k, k_a,
           r_k, lnx_g, lnx_b, q_norm_g, w_uq, kv_norm_g, w_ukv, w_out, ln1_g, ln1_b, w_ffn1,
           w_ffn2, ln2_g, ln2_b, alpha):
    B, S, D = x.shape
    T = B * S
    H = MLA_HEADS
    tm = _pick(S, 512)
    tb = _pick(S, 512)
    chunk = _pick(tb, CHUNK)
    nt = S // tm

    w_r = w_in[:, :RWKV_COLS]
    w_m = w_in[:, RWKV_COLS:]
    w_pe = jnp.zeros((D, LANES), F32).at[:, PE_LO:PE_LO + MLA_ROPE].set(w_m[:, Q_LORA + KV_LORA:])
    w_all = jnp.concatenate([w_r, w_m[:, :Q_LORA + KV_LORA], w_pe], axis=1).astype(BF16)
    mu = shift_mu.reshape(1, RWKV_COLS)
    lora = jnp.zeros((LANES, 2 * RWKV_WIDTH), F32)
    lora = lora.at[:DECAY_LORA, :RWKV_WIDTH].set(decay_w2).at[DECAY_LORA:, RWKV_WIDTH:].set(iclr_a2)
    lora = lora.astype(BF16)
    w0a0 = jnp.concatenate([decay_w0, iclr_a0]).reshape(1, 2 * RWKV_WIDTH)
    g2 = gate_g2.astype(BF16)
    dq = MLA_NOPE + MLA_ROPE
    wq = jnp.zeros((Q_LORA, H, LANES), F32).at[:, :, :dq].set(w_uq.reshape(Q_LORA, H, dq))
    wq = wq.reshape(Q_LORA, H * LANES).astype(BF16)
    wkv = w_ukv.reshape(KV_LORA, H, MLA_NOPE + MLA_V)
    wk = jnp.zeros((KV_LORA, H, LANES), F32).at[:, :, :MLA_NOPE].set(wkv[:, :, :MLA_NOPE])
    wk = wk.reshape(KV_LORA, H * LANES).astype(BF16)
    wvt = wkv[:, :, MLA_NOPE:].reshape(KV_LORA, H * MLA_V).T.astype(BF16)
    inv_freq = ROPE_BASE ** (-jnp.arange(0, MLA_ROPE, 2, dtype=F32) / MLA_ROPE)
    freq = jnp.zeros((1, LANES), F32).at[0, PE_LO:PE_LO + PE_HALF].set(inv_freq)
    freq = freq.at[0, PE_LO + PE_HALF:PE_LO + MLA_ROPE].set(inv_freq)
    sgn = jnp.zeros((1, LANES), F32).at[0, PE_LO:PE_LO + PE_HALF].set(-1.0)
    sgn = sgn.at[0, PE_LO + PE_HALF:PE_LO + MLA_ROPE].set(1.0)
    par = jnp.zeros((8, RWKV_WIDTH), F32)
    par = par.at[0].set(k_k).at[1].set(k_a).at[2].set(r_k.reshape(-1)).at[3].set(lnx_g).at[4].set(lnx_b)
    ln1 = jnp.stack([ln1_g, ln1_b])
    ln2 = jnp.stack([ln2_g, ln2_b])
    wo_r = w_out[:RWKV_WIDTH].astype(BF16)
    wo_m = w_out[RWKV_WIDTH:].astype(BF16)
    w1 = w_ffn1.astype(BF16)
    w2 = w_ffn2.astype(BF16)

    pos3 = positions.reshape(B, S, 1)

    tok = lambda: pl.BlockSpec((None, tm, RWKV_WIDTH), lambda b, j: (b, j, 0))
    tok_shape = jax.ShapeDtypeStruct((B, S, RWKV_WIDTH), F32)
    head_spec = pl.BlockSpec((None, H, tm, LANES), lambda b, j: (b, 0, j, 0))
    r, k, v, lw, a, g, q, kh, vt = pl.pallas_call(
        functools.partial(_prep_kernel, scale=float(dq) ** -0.5 * math.log2(math.e)),
        grid=(B, nt),
        in_specs=[
            pl.BlockSpec((None, tm, D), lambda b, j: (b, j, 0)),
            pl.BlockSpec((None, tm, 1), lambda b, j: (b, j, 0)),
            _const_spec(w_all.shape), _const_spec(mu.shape), _const_spec(lora.shape),
            _const_spec(w0a0.shape), _const_spec(g2.shape),
            _const_spec((1, Q_LORA)), _const_spec(wq.shape),
            _const_spec((1, KV_LORA)), _const_spec(wk.shape), _const_spec(wvt.shape),
            _const_spec(freq.shape), _const_spec(sgn.shape),
        ],
        out_specs=[tok(), tok(), tok(), tok(), tok(), tok(), head_spec, head_spec,
                   pl.BlockSpec((None, None, H * MLA_V, tm), lambda b, j: (b, j, 0, 0))],
        out_shape=[tok_shape] * 6 + [
            jax.ShapeDtypeStruct((B, H, S, LANES), BF16),
            jax.ShapeDtypeStruct((B, H, S, LANES), BF16),
            jax.ShapeDtypeStruct((B, nt, H * MLA_V, tm), BF16)],
        scratch_shapes=[pltpu.VMEM((8, RWKV_COLS), F32)],
        compiler_params=pltpu.CompilerParams(
            dimension_semantics=("arbitrary", "arbitrary"), vmem_limit_bytes=VMEM_LIMIT),
    )(x, pos3, w_all, mu, lora, w0a0, g2, q_norm_g.reshape(1, Q_LORA), wq,
      kv_norm_g.reshape(1, KV_LORA), wk, wvt, freq, sgn)

    npair = RWKV_WIDTH // LANES
    nsteps = S // tb
    nblocks = B * npair * nsteps
    nc = tb // chunk

    def block_of(s):
        return s // (npair * nsteps), s % nsteps, (s // nsteps) % npair

    cur = lambda: pl.BlockSpec((None, tb, LANES), lambda s: block_of(jnp.minimum(s, nblocks - 1)))
    prev = lambda: pl.BlockSpec((None, tb, LANES), lambda s: block_of(jnp.maximum(s - 1, 0)))
    y_rwkv = pl.pallas_call(
        functools.partial(_rwkv_kernel, chunk=chunk, nsteps=nsteps),
        grid=(nblocks + 1,),
        in_specs=[cur(), cur(), cur(), cur(), cur(), prev(),
                  pl.BlockSpec((8, LANES), lambda s: (0, block_of(jnp.minimum(s, nblocks - 1))[2])),
                  pl.BlockSpec((8, LANES), lambda s: (0, block_of(jnp.maximum(s - 1, 0))[2]))],
        out_specs=prev(),
        out_shape=jax.ShapeDtypeStruct((B, S, RWKV_WIDTH), BF16),
        scratch_shapes=[pltpu.VMEM((LANES, LANES), F32),
                        pltpu.VMEM((nc, 2 * chunk, LANES), BF16),
                        pltpu.VMEM((nc, 2 * chunk, LANES), F32),
                        pltpu.VMEM((nc, LANES, LANES), BF16),
                        pltpu.VMEM((nc, LANES, LANES), F32),
                        pltpu.VMEM((tb, LANES), F32)],
        compiler_params=pltpu.CompilerParams(
            dimension_semantics=("arbitrary",), vmem_limit_bytes=VMEM_LIMIT),
    )(r, k, v, lw, a, g, par, par)

    y_mla_t = pl.pallas_call(
        _attn_kernel,
        grid=(B, H // ATTN_GROUP, nt),
        in_specs=[
            pl.BlockSpec((None, ATTN_GROUP, tm, LANES), lambda b, h, i: (b, h, i, 0)),
            pl.BlockSpec((None, ATTN_GROUP, S, LANES), lambda b, h, i: (b, h, 0, 0)),
            pl.BlockSpec((None, nt, ATTN_GROUP * MLA_V, tm), lambda b, h, i: (b, 0, h, 0)),
        ],
        out_specs=pl.BlockSpec((None, ATTN_GROUP * MLA_V, tm), lambda b, h, i: (b, h, i)),
        out_shape=jax.ShapeDtypeStruct((B, H * MLA_V, S), BF16),
        scratch_shapes=[pltpu.VMEM((ATTN_GROUP, 8, LANES), F32),
                        pltpu.VMEM((2, ATTN_GROUP, tm, tm), F32),
                        pltpu.VMEM((ATTN_GROUP, MLA_V + ATTN_ONES_ROWS, tm), F32)],
        compiler_params=pltpu.CompilerParams(
            dimension_semantics=("arbitrary", "arbitrary", "arbitrary"),
            vmem_limit_bytes=VMEM_LIMIT),
    )(q, kh, vt)

    out = pl.pallas_call(
        functools.partial(_out_kernel, alpha=alpha, ff_chunk=_pick(D_FF, 1024)),
        grid=(B, nt),
        in_specs=[
            pl.BlockSpec((None, tm, D), lambda b, j: (b, j, 0)),
            pl.BlockSpec((None, tm, RWKV_WIDTH), lambda b, j: (b, j, 0)),
            pl.BlockSpec((None, MLA_WIDTH, tm), lambda b, j: (b, 0, j)),
            _const_spec(wo_r.shape), _const_spec(wo_m.shape), _const_spec(ln1.shape),
            _const_spec(w1.shape), _const_spec(w2.shape), _const_spec(ln2.shape),
        ],
        out_specs=pl.BlockSpec((None, tm, D), lambda b, j: (b, j, 0)),
        out_shape=jax.ShapeDtypeStruct((B, S, D), F32),
        compiler_params=pltpu.CompilerParams(
            dimension_semantics=("arbitrary", "arbitrary"), vmem_limit_bytes=VMEM_LIMIT),
    )(x, y_rwkv, y_mla_t, wo_r, wo_m, ln1, w1, w2, ln2)
    return out


def kernel(x, positions, w_in, shift_mu, decay_w0, decay_w2, iclr_a0, iclr_a2, gate_g2, k_k, k_a, r_k, lnx_g, lnx_b, q_norm_g, w_uq, kv_norm_g, w_ukv, w_out, ln1_g, ln1_b, w_ffn1, w_ffn2, ln2_g, ln2_b):
    depth = w_in.shape[0]
    alpha = (2 * depth) ** 0.25
    params = (w_in, shift_mu, decay_w0, decay_w2, iclr_a0, iclr_a2, gate_g2, k_k, k_a, r_k, lnx_g,
              lnx_b, q_norm_g, w_uq, kv_norm_g, w_ukv, w_out, ln1_g, ln1_b, w_ffn1, w_ffn2,
              ln2_g, ln2_b)
    for l in range(depth):
        x = _layer(x, positions, *(p[l] for p in params), alpha)
    return x
```

```python
import functools
import math

import jax
import jax.numpy as jnp
from jax import lax
from jax.experimental import pallas as pl
from jax.experimental.pallas import tpu as pltpu

F32 = jnp.float32
BF16 = jnp.bfloat16

D_MODEL = 1024
RWKV_HEAD = 64
RWKV_HEADS = 8
RWKV_WIDTH = RWKV_HEADS * RWKV_HEAD
DECAY_LORA = 64
ICLR_LORA = 64
GATE_LORA = 128
MLA_HEADS = 8
MLA_NOPE = 64
MLA_ROPE = 32
MLA_V = 64
MLA_WIDTH = MLA_HEADS * MLA_V
Q_LORA = 384
KV_LORA = 256
ROPE_BASE = 10000.0
D_FF = 4 * D_MODEL
LN_EPS = 1e-5
RMS_EPS = 1e-6
GN_EPS = 64e-5
RWKV_COLS = 3 * RWKV_WIDTH + DECAY_LORA + ICLR_LORA + GATE_LORA
MLA_COLS = Q_LORA + KV_LORA + MLA_ROPE

LANES = 128
VMEM_LIMIT = 56 * 1024 * 1024

C_R, C_K, C_V = 0, RWKV_WIDTH, 2 * RWKV_WIDTH
C_WA = 3 * RWKV_WIDTH
C_G = C_WA + DECAY_LORA + ICLR_LORA
C_Q = RWKV_COLS
C_KV = C_Q + Q_LORA
C_PE = C_KV + KV_LORA
IN_COLS_PAD = C_PE + LANES
PE_LO = MLA_NOPE
PE_HALF = MLA_ROPE // 2

CHUNK = 64
ATTN_GROUP = 4
ATTN_ONES_ROWS = 16
F32_MAX_EXP2 = 126.0
ATTN_BOUND_SLACK = 1.01


def _dot(a, b):
    return jnp.dot(a.astype(BF16), b.astype(BF16), preferred_element_type=F32)


def _split_bf16(x):
    hi = x.astype(BF16)
    lo = (x - hi.astype(F32)).astype(BF16)
    return hi, lo


def _sigmoid(z):
    return 1.0 / (1.0 + jnp.exp(-z))


def _layer_norm(x, g, b):
    mu = jnp.mean(x, axis=-1, keepdims=True)
    d = x - mu
    var = jnp.mean(d * d, axis=-1, keepdims=True)
    return d * lax.rsqrt(var + LN_EPS) * g + b


def _prep_kernel(x_ref, pos_ref, w_ref, mu_ref, lora_ref, w0a0_ref, g2_ref, qg_ref, wq_ref,
                 kvg_ref, wk_ref, wvt_ref, freq_ref, sgn_ref,
                 r_ref, k_ref, v_ref, lw_ref, a_ref, g_ref, q_ref, kh_ref, vt_ref, carry_ref,
                 *, scale):
    tm = x_ref.shape[0]
    j = pl.program_id(1)

    @pl.when(j == 0)
    def _():
        carry_ref[...] = jnp.zeros_like(carry_ref)

    h = jnp.dot(x_ref[...].astype(BF16), w_ref[...], preferred_element_type=F32)

    hr = h[:, :RWKV_COLS]
    row = lax.broadcasted_iota(jnp.int32, hr.shape, 0)
    prev = jnp.where(row == 0, carry_ref[0:1, :], pltpu.roll(hr, 1, axis=0))
    carry_ref[0:1, :] = hr[tm - 1:tm, :]
    hr = hr + (prev - hr) * mu_ref[...]

    r_ref[...] = hr[:, C_R:C_R + RWKV_WIDTH]
    k_ref[...] = hr[:, C_K:C_K + RWKV_WIDTH]
    v_ref[...] = hr[:, C_V:C_V + RWKV_WIDTH]

    lane = lax.broadcasted_iota(jnp.int32, (tm, LANES), 1)
    xwa = hr[:, C_WA:C_WA + LANES]
    xwa = jnp.where(lane < DECAY_LORA, jnp.tanh(xwa), xwa)
    z = _dot(xwa, lora_ref[...]) + w0a0_ref[...]
    sg = _sigmoid(z)
    lw_ref[...] = (-math.exp(-0.5)) * sg[:, :RWKV_WIDTH]
    a_ref[...] = sg[:, RWKV_WIDTH:]
    g_ref[...] = _dot(_sigmoid(hr[:, C_G:C_G + GATE_LORA]), g2_ref[...])

    def rms(c, gain):
        return c * lax.rsqrt(jnp.mean(c * c, axis=-1, keepdims=True) + RMS_EPS) * gain

    cqn = rms(h[:, C_Q:C_Q + Q_LORA], qg_ref[...]).astype(BF16)
    ckvn = rms(h[:, C_KV:C_KV + KV_LORA], kvg_ref[...]).astype(BF16)
    q_all = jnp.dot(cqn, wq_ref[...], preferred_element_type=F32)
    k_all = jnp.dot(ckvn, wk_ref[...], preferred_element_type=F32)
    vt = lax.dot_general(wvt_ref[...], ckvn, (((1,), (1,)), ((), ())),
                         preferred_element_type=F32)
    vt_ref[...] = vt.astype(vt_ref.dtype)

    ang = pos_ref[...].astype(F32) * freq_ref[...]
    cosf = jnp.cos(ang)
    sins = jnp.sin(ang) * sgn_ref[...]
    first_half = (lane >= PE_LO) & (lane < PE_LO + PE_HALF)

    def rope(zg):
        rot = jnp.where(first_half, pltpu.roll(zg, LANES - PE_HALF, axis=1),
                        pltpu.roll(zg, PE_HALF, axis=1))
        return zg * cosf + rot * sins

    kpe = rope(h[:, C_PE:C_PE + LANES])
    for hd in range(MLA_HEADS):
        sl = slice(hd * LANES, (hd + 1) * LANES)
        q_ref[hd] = (rope(q_all[:, sl]) * scale).astype(q_ref.dtype)
        kh_ref[hd] = (k_all[:, sl] + kpe).astype(kh_ref.dtype)


_NN = (((2,), (1,)), ((0,), (0,)))
_NT = (((2,), (2,)), ((0,), (0,)))
_TN = (((1,), (1,)), ((0,), (0,)))


def _rwkv_kernel(r_ref, k_ref, v_ref, lw_ref, a_ref, gprev_ref, par_ref, parprev_ref, o_ref,
                 h_ref, rq_ref, ya_ref, g2_ref, ha_ref, bonus_ref, *, chunk, nsteps):
    tb = r_ref.shape[0]
    C = chunk
    nc = tb // C
    C2 = 2 * C
    s = pl.program_id(0)

    @pl.when(s == 0)
    def _():
        for ref in (h_ref, rq_ref, ya_ref, g2_ref, ha_ref, bonus_ref):
            ref[...] = jnp.zeros_like(ref)

    first_of_sequence = (jnp.maximum(s - 1, 0) % nsteps) == 0
    scan = {"h": h_ref[...] * jnp.where(first_of_sequence, 0.0, 1.0), "ys": []}

    def scan_step():
        c = len(scan["ys"])
        if c < nc:
            hb = scan["h"].astype(BF16)
            y2 = jnp.dot(rq_ref[c], hb, preferred_element_type=F32) + ya_ref[c]
            scan["ys"].append(y2[:C] + y2[C:])
            scan["h"] = jnp.dot(g2_ref[c], hb, preferred_element_type=F32) + ha_ref[c]

    lane = lax.broadcasted_iota(jnp.int32, (1, LANES), 1)
    m0 = (lane < RWKV_HEAD).astype(F32)
    m1 = 1.0 - m0
    rr = lax.broadcasted_iota(jnp.int32, (LANES, LANES), 0)
    cc = lax.broadcasted_iota(jnp.int32, (LANES, LANES), 1)
    ones2 = ((rr < RWKV_HEAD) == (cc < RWKV_HEAD)).astype(BF16)
    eye = (rr == cc).astype(F32)

    k_k, k_a, r_k = par_ref[0:1, :], par_ref[1:2, :], par_ref[2:3, :]
    ln_g, ln_b = parprev_ref[3:4, :], parprev_ref[4:5, :]

    r = r_ref[...]
    k = k_ref[...]
    v = v_ref[...]
    lw = lw_ref[...]
    a = a_ref[...]

    kk0 = k * k_k
    n2 = _dot(kk0 * kk0, ones2)
    kk = kk0 / jnp.maximum(jnp.sqrt(n2), 1e-12)
    kmod = k * (1.0 + (a - 1.0) * k_a)
    bonus = _dot(r * kmod * r_k, ones2) * v
    bb = kk * a

    to3 = lambda t: t.reshape(nc, C, LANES)
    r3, k3, v3, lw3, kk3, bb3 = to3(r), to3(kmod), to3(v), to3(lw), to3(kk), to3(bb)

    ti = lax.broadcasted_iota(jnp.int32, (nc, C, C), 1)
    si = lax.broadcasted_iota(jnp.int32, (nc, C, C), 2)
    tril = (si <= ti).astype(BF16)
    lhi, llo = _split_bf16(lw3)
    cum = (lax.dot_general(tril, lhi, _NN, preferred_element_type=F32)
           + lax.dot_general(tril, llo, _NN, preferred_element_type=F32))
    cend = cum[:, C - 1:C, :]
    p_inc = jnp.exp(cum)
    p_exc = jnp.exp(cum - lw3)
    p_inv = jnp.exp(-cum)
    p_end = jnp.exp(cend - cum)
    p_all = jnp.exp(cend)

    m0b, m1b = m0.astype(BF16), m1.astype(BF16)

    def expand(t):
        tb = t.astype(BF16)
        return jnp.concatenate([tb * m0b, tb * m1b], axis=1)

    rt = r3 * p_inc
    at2 = expand(-kk3 * p_exc)
    rt2 = expand(rt)
    bt2 = expand(bb3 * p_inv)
    kt2 = expand(k3 * p_inv)
    v2 = expand(v3)
    bbar2 = expand(bb3 * p_end)
    kbar2 = expand(k3 * p_end)

    bdot = functools.partial(lax.dot_general, preferred_element_type=F32)
    lhs = jnp.concatenate([at2, rt2], axis=1)
    rhs = jnp.concatenate([bt2, kt2], axis=1)
    m_all = bdot(lhs, rhs, _NT)
    ri = lax.broadcasted_iota(jnp.int32, (4 * C, 4 * C), 0)
    ci = lax.broadcasted_iota(jnp.int32, (4 * C, 4 * C), 1)
    bottom = (ri >= C2).astype(jnp.int32)
    right = (ci >= C2).astype(jnp.int32)
    keep = (ri - C2 * bottom) - (ci - C2 * right) + bottom > 0
    m_all = jnp.where(keep[None], m_all, 0.0)
    a_ab = m_all[:, :C2, :C2]
    a_ak = m_all[:, :C2, C2:].astype(BF16)
    a_r = m_all[:, C2:, :].astype(BF16)
    scan_step()

    rounds = max(1, (C - 1).bit_length()) - 1
    tmat = eye[None] + a_ab
    pw = a_ab.astype(BF16)
    pw = bdot(pw, pw, _NN).astype(BF16)
    scan_step()
    for rnd in range(rounds):
        if rnd + 1 < rounds:
            both = bdot(pw, jnp.concatenate([tmat.astype(BF16), pw], axis=2), _NN)
            tmat = tmat + both[:, :, :C2]
            pw = both[:, :, C2:].astype(BF16)
        else:
            tmat = tmat + bdot(pw, tmat.astype(BF16), _NN)
        scan_step()

    akv = bdot(a_ak, v2, _NN).astype(BF16)
    scan_step()
    wu = bdot(tmat.astype(BF16), jnp.concatenate([at2, akv], axis=2), _NN)
    wu = wu.astype(BF16)
    while len(scan["ys"]) < nc:
        scan_step()
    h_ref[...] = scan["h"]

    y = jnp.concatenate(scan["ys"], axis=0)
    inv_n = 1.0 / RWKV_HEAD
    mu = _dot(y, ones2) * inv_n
    d = y - mu
    var = _dot(d * d, ones2) * inv_n
    yn = d * lax.rsqrt(var + GN_EPS) * ln_g + ln_b
    o_ref[...] = ((yn + bonus_ref[...]) * gprev_ref[...]).astype(o_ref.dtype)

    zmat = jnp.concatenate(
        [wu, jnp.concatenate([jnp.zeros_like(v2), v2], axis=2)], axis=1)
    ry = bdot(a_r, zmat, _NN)
    bk = jnp.concatenate([bbar2, kbar2], axis=1)
    gh = bdot(bk, zmat, _TN)

    rq_ref[...] = (jnp.concatenate([rt * m0, rt * m1], axis=1) + ry[:, :, :LANES]).astype(BF16)
    ya_ref[...] = ry[:, :, LANES:]
    g2_ref[...] = (gh[:, :, :LANES] + eye[None] * p_all).astype(BF16)
    ha_ref[...] = gh[:, :, LANES:]
    bonus_ref[...] = bonus


def _attn_kernel(q_ref, k_ref, vt_ref, o_ref, kmax_ref, vlog_ref, st_ref, acc_ref):
    nheads, t, _ = q_ref.shape
    seq = k_ref.shape[1]
    i = pl.program_id(2)
    dn = (((1,), (1,)), ((), ()))
    sub = 8

    @pl.when(i == 0)
    def _():
        for hd in range(nheads):
            ka = jnp.abs(k_ref[hd].astype(F32)).reshape(seq // sub, sub, LANES)
            kmax_ref[hd] = jnp.max(ka, axis=0)
        vmax = jnp.max(jnp.abs(vt_ref[...].astype(F32)))
        vlog_ref[...] = jnp.log2(jnp.maximum(jnp.full(vlog_ref.shape, vmax), 1.0))

    def scores(hd, jb):
        kb = k_ref[hd, pl.ds(pl.multiple_of(jb * t, t), t), :]
        return lax.dot_general(kb, q_ref[hd], dn, preferred_element_type=F32)

    def causal(x):
        krow = lax.broadcasted_iota(jnp.int32, x.shape, 0)
        qcol = lax.broadcasted_iota(jnp.int32, x.shape, 1)
        return jnp.where(krow <= qcol, x, -jnp.inf)

    worst = []
    for hd in range(nheads):
        bound = lax.dot_general(kmax_ref[hd].astype(BF16), jnp.abs(q_ref[hd]), dn,
                                preferred_element_type=F32)
        worst.append(jnp.max(bound) * ATTN_BOUND_SLACK)
    budget = F32_MAX_EXP2 - math.log2(seq)
    fast_ok = functools.reduce(jnp.maximum, worst) + jnp.max(vlog_ref[...]) <= budget

    @pl.when(fast_ok)
    def _():
        ones_v = jnp.ones((ATTN_ONES_ROWS, t), BF16)
        acc_ref[...] = jnp.zeros_like(acc_ref)

        def produce(jb, slot):
            for hd in range(nheads):
                st_ref[slot, hd] = scores(hd, jb)

        def consume(jb, slot, masked):
            for hd in range(nheads):
                x = st_ref[slot, hd]
                p = jnp.exp2(causal(x) if masked else x).astype(BF16)
                vaug = jnp.concatenate(
                    [vt_ref[jb, hd * MLA_V:(hd + 1) * MLA_V, :], ones_v], axis=0)
                acc_ref[hd] += jnp.dot(vaug, p, preferred_element_type=F32)

        produce(0, 0)

        def pair(jj, c):
            j0 = 2 * jj
            produce(j0 + 1, 1)
            consume(j0, 0, False)
            produce(j0 + 2, 0)
            consume(j0 + 1, 1, False)
            return c

        lax.fori_loop(0, i // 2, pair, 0)

        @pl.when(i % 2 == 1)
        def _():
            produce(i, 1)
            consume(i - 1, 0, False)
            consume(i, 1, True)

        @pl.when(i % 2 == 0)
        def _():
            consume(i, 0, True)

        for hd in range(nheads):
            acc = acc_ref[hd]
            o_ref[hd * MLA_V:(hd + 1) * MLA_V, :] = (
                acc[:MLA_V] / acc[MLA_V:MLA_V + 1]).astype(o_ref.dtype)

    @pl.when(jnp.logical_not(fast_ok))
    def _():
        def block(jb, carry, masked):
            out = []
            for hd in range(nheads):
                m, l, acc = carry[hd]
                st = scores(hd, jb)
                if masked:
                    st = causal(st)
                mn = jnp.maximum(m, jnp.max(st, axis=0, keepdims=True))
                alpha = jnp.exp2(m - mn)
                p = jnp.exp2(st - mn)
                l = l * alpha + jnp.sum(p, axis=0, keepdims=True)
                vb = vt_ref[jb, hd * MLA_V:(hd + 1) * MLA_V, :]
                acc = acc * alpha + jnp.dot(vb, p.astype(BF16), preferred_element_type=F32)
                out.append((mn, l, acc))
            return tuple(out)

        init = tuple((jnp.full((1, t), -jnp.inf, F32), jnp.zeros((1, t), F32),
                      jnp.zeros((MLA_V, t), F32)) for _ in range(nheads))
        carry = lax.fori_loop(0, i, lambda jb, c: block(jb, c, False), init)
        carry = block(i, carry, True)
        for hd in range(nheads):
            _, l, acc = carry[hd]
            o_ref[hd * MLA_V:(hd + 1) * MLA_V, :] = (acc / l).astype(o_ref.dtype)


def _out_kernel(x_ref, yr_ref, ymt_ref, wor_ref, wom_ref, ln1_ref, w1_ref, w2_ref, ln2_ref, o_ref,
                *, alpha, ff_chunk):
    x = x_ref[...]
    mix = jnp.dot(yr_ref[...], wor_ref[...], preferred_element_type=F32)
    mix = mix + lax.dot_general(ymt_ref[...], wom_ref[...], (((0,), (0,)), ((), ())),
                                preferred_element_type=F32)
    x1 = _layer_norm(alpha * x + mix, ln1_ref[0:1, :], ln1_ref[1:2, :])
    x1b = x1.astype(BF16)
    f = jnp.zeros_like(x1)
    for c in range(0, w1_ref.shape[1], ff_chunk):
        hid = jnp.dot(x1b, w1_ref[:, c:c + ff_chunk], preferred_element_type=F32)
        hid = jnp.square(jnp.maximum(hid, 0.0)).astype(BF16)
        f = f + jnp.dot(hid, w2_ref[c:c + ff_chunk, :], preferred_element_type=F32)
    o_ref[...] = _layer_norm(alpha * x1 + f, ln2_ref[0:1, :], ln2_ref[1:2, :])


def _const_spec(shape):
    nd = len(shape)
    return pl.BlockSpec(shape, lambda *_: (0,) * nd, pipeline_mode=pl.Buffered(1))


def _pick(n, pref):
    t = min(pref, n)
    while n % t:
        t //= 2
    return t


def _layer(x, positions, w_in, shift_mu, decay_w0, decay_w2, iclr_a0, iclr_a2, gate_g2, k_k, k_a,
           r_k, lnx_g, lnx_b, q_norm_g, w_uq, kv_norm_g, w_ukv, w_out, ln1_g, ln1_b, w_ffn1,
           w_ffn2, ln2_g, ln2_b, alpha):
    B, S, D = x.shape
    T = B * S
    H = MLA_HEADS
    tm = _pick(S, 512)
    tb = _pick(S, 512)
    chunk = _pick(tb, CHUNK)
    nt = S // tm

    w_r = w_in[:, :RWKV_COLS]
    w_m = w_in[:, RWKV_COLS:]
    w_pe = jnp.zeros((D, LANES), F32).at[:, PE_LO:PE_LO + MLA_ROPE].set(w_m[:, Q_LORA + KV_LORA:])
    w_all = jnp.concatenate([w_r, w_m[:, :Q_LORA + KV_LORA], w_pe], axis=1).astype(BF16)
    mu = shift_mu.reshape(1, RWKV_COLS)
    lora = jnp.zeros((LANES, 2 * RWKV_WIDTH), F32)
    lora = lora.at[:DECAY_LORA, :RWKV_WIDTH].set(decay_w2).at[DECAY_LORA:, RWKV_WIDTH:].set(iclr_a2)
    lora = lora.astype(BF16)
    w0a0 = jnp.concatenate([decay_w0, iclr_a0]).reshape(1, 2 * RWKV_WIDTH)
    g2 = gate_g2.astype(BF16)
    dq = MLA_NOPE + MLA_ROPE
    wq = jnp.zeros((Q_LORA, H, LANES), F32).at[:, :, :dq].set(w_uq.reshape(Q_LORA, H, dq))
    wq = wq.reshape(Q_LORA, H * LANES).astype(BF16)
    wkv = w_ukv.reshape(KV_LORA, H, MLA_NOPE + MLA_V)
    wk = jnp.zeros((KV_LORA, H, LANES), F32).at[:, :, :MLA_NOPE].set(wkv[:, :, :MLA_NOPE])
    wk = wk.reshape(KV_LORA, H * LANES).astype(BF16)
    wvt = wkv[:, :, MLA_NOPE:].reshape(KV_LORA, H * MLA_V).T.astype(BF16)
    inv_freq = ROPE_BASE ** (-jnp.arange(0, MLA_ROPE, 2, dtype=F32) / MLA_ROPE)
    freq = jnp.zeros((1, LANES), F32).at[0, PE_LO:PE_LO + PE_HALF].set(inv_freq)
    freq = freq.at[0, PE_LO + PE_HALF:PE_LO + MLA_ROPE].set(inv_freq)
    sgn = jnp.zeros((1, LANES), F32).at[0, PE_LO:PE_LO + PE_HALF].set(-1.0)
    sgn = sgn.at[0, PE_LO + PE_HALF:PE_LO + MLA_ROPE].set(1.0)
    par = jnp.zeros((8, RWKV_WIDTH), F32)
    par = par.at[0].set(k_k).at[1].set(k_a).at[2].set(r_k.reshape(-1)).at[3].set(lnx_g).at[4].set(lnx_b)
    ln1 = jnp.stack([ln1_g, ln1_b])
    ln2 = jnp.stack([ln2_g, ln2_b])
    wo_r = w_out[:RWKV_WIDTH].astype(BF16)
    wo_m = w_out[RWKV_WIDTH:].astype(BF16)
    w1 = w_ffn1.astype(BF16)
    w2 = w_ffn2.astype(BF16)

    pos3 = positions.reshape(B, S, 1)

    tok = lambda: pl.BlockSpec((None, tm, RWKV_WIDTH), lambda b, j: (b, j, 0))
    tok_shape = jax.ShapeDtypeStruct((B, S, RWKV_WIDTH), F32)
    head_spec = pl.BlockSpec((None, H, tm, LANES), lambda b, j: (b, 0, j, 0))
    r, k, v, lw, a, g, q, kh, vt = pl.pallas_call(
        functools.partial(_prep_kernel, scale=float(dq) ** -0.5 * math.log2(math.e)),
        grid=(B, nt),
        in_specs=[
            pl.BlockSpec((None, tm, D), lambda b, j: (b, j, 0)),
            pl.BlockSpec((None, tm, 1), lambda b, j: (b, j, 0)),
            _const_spec(w_all.shape), _const_spec(mu.shape), _const_spec(lora.shape),
            _const_spec(w0a0.shape), _const_spec(g2.shape),
            _const_spec((1, Q_LORA)), _const_spec(wq.shape),
            _const_spec((1, KV_LORA)), _const_spec(wk.shape), _const_spec(wvt.shape),
            _const_spec(freq.shape), _const_spec(sgn.shape),
        ],
        out_specs=[tok(), tok(), tok(), tok(), tok(), tok(), head_spec, head_spec,
                   pl.BlockSpec((None, None, H * MLA_V, tm), lambda b, j: (b, j, 0, 0))],
        out_shape=[tok_shape] * 6 + [
            jax.ShapeDtypeStruct((B, H, S, LANES), BF16),
            jax.ShapeDtypeStruct((B, H, S, LANES), BF16),
            jax.ShapeDtypeStruct((B, nt, H * MLA_V, tm), BF16)],
        scratch_shapes=[pltpu.VMEM((8, RWKV_COLS), F32)],
        compiler_params=pltpu.CompilerParams(
            dimension_semantics=("arbitrary", "arbitrary"), vmem_limit_bytes=VMEM_LIMIT),
    )(x, pos3, w_all, mu, lora, w0a0, g2, q_norm_g.reshape(1, Q_LORA), wq,
      kv_norm_g.reshape(1, KV_LORA), wk, wvt, freq, sgn)

    npair = RWKV_WIDTH // LANES
    nsteps = S // tb
    nblocks = B * npair * nsteps
    nc = tb // chunk

    def block_of(s):
        return s // (npair * nsteps), s % nsteps, (s // nsteps) % npair

    cur = lambda: pl.BlockSpec((None, tb, LANES), lambda s: block_of(jnp.minimum(s, nblocks - 1)))
    prev = lambda: pl.BlockSpec((None, tb, LANES), lambda s: block_of(jnp.maximum(s - 1, 0)))
    y_rwkv = pl.pallas_call(
        functools.partial(_rwkv_kernel, chunk=chunk, nsteps=nsteps),
        grid=(nblocks + 1,),
        in_specs=[cur(), cur(), cur(), cur(), cur(), prev(),
                  pl.BlockSpec((8, LANES), lambda s: (0, block_of(jnp.minimum(s, nblocks - 1))[2])),
                  pl.BlockSpec((8, LANES), lambda s: (0, block_of(jnp.maximum(s - 1, 0))[2]))],
        out_specs=prev(),
        out_shape=jax.ShapeDtypeStruct((B, S, RWKV_WIDTH), BF16),
        scratch_shapes=[pltpu.VMEM((LANES, LANES), F32),
                        pltpu.VMEM((nc, 2 * chunk, LANES), BF16),
                        pltpu.VMEM((nc, 2 * chunk, LANES), F32),
                        pltpu.VMEM((nc, LANES, LANES), BF16),
                        pltpu.VMEM((nc, LANES, LANES), F32),
                        pltpu.VMEM((tb, LANES), F32)],
        compiler_params=pltpu.CompilerParams(
            dimension_semantics=("arbitrary",), vmem_limit_bytes=VMEM_LIMIT),
    )(r, k, v, lw, a, g, par, par)

    y_mla_t = pl.pallas_call(
        _attn_kernel,
        grid=(B, H // ATTN_GROUP, nt),
        in_specs=[
            pl.BlockSpec((None, ATTN_GROUP, tm, LANES), lambda b, h, i: (b, h, i, 0)),
            pl.BlockSpec((None, ATTN_GROUP, S, LANES), lambda b, h, i: (b, h, 0, 0)),
            pl.BlockSpec((None, nt, ATTN_GROUP * MLA_V, tm), lambda b, h, i: (b, 0, h, 0)),
        ],
        out_specs=pl.BlockSpec((None, ATTN_GROUP * MLA_V, tm), lambda b, h, i: (b, h, i)),
        out_shape=jax.ShapeDtypeStruct((B, H * MLA_V, S), BF16),
        scratch_shapes=[pltpu.VMEM((ATTN_GROUP, 8, LANES), F32),
                        pltpu.VMEM((8, LANES), F32),
                        pltpu.VMEM((2, ATTN_GROUP, tm, tm), F32),
                        pltpu.VMEM((ATTN_GROUP, MLA_V + ATTN_ONES_ROWS, tm), F32)],
        compiler_params=pltpu.CompilerParams(
            dimension_semantics=("arbitrary", "arbitrary", "arbitrary"),
            vmem_limit_bytes=VMEM_LIMIT),
    )(q, kh, vt)

    out = pl.pallas_call(
        functools.partial(_out_kernel, alpha=alpha, ff_chunk=_pick(D_FF, 1024)),
        grid=(B, nt),
        in_specs=[
            pl.BlockSpec((None, tm, D), lambda b, j: (b, j, 0)),
            pl.BlockSpec((None, tm, RWKV_WIDTH), lambda b, j: (b, j, 0)),
            pl.BlockSpec((None, MLA_WIDTH, tm), lambda b, j: (b, 0, j)),
            _const_spec(wo_r.shape), _const_spec(wo_m.shape), _const_spec(ln1.shape),
            _const_spec(w1.shape), _const_spec(w2.shape), _const_spec(ln2.shape),
        ],
        out_specs=pl.BlockSpec((None, tm, D), lambda b, j: (b, j, 0)),
        out_shape=jax.ShapeDtypeStruct((B, S, D), F32),
        compiler_params=pltpu.CompilerParams(
            dimension_semantics=("arbitrary", "arbitrary"), vmem_limit_bytes=VMEM_LIMIT),
    )(x, y_rwkv, y_mla_t, wo_r, wo_m, ln1, w1, w2, ln2)
    return out


def kernel(x, positions, w_in, shift_mu, decay_w0, decay_w2, iclr_a0, iclr_a2, gate_g2, k_k, k_a, r_k, lnx_g, lnx_b, q_norm_g, w_uq, kv_norm_g, w_ukv, w_out, ln1_g, ln1_b, w_ffn1, w_ffn2, ln2_g, ln2_b):
    depth = w_in.shape[0]
    alpha = (2 * depth) ** 0.25
    params = (w_in, shift_mu, decay_w0, decay_w2, iclr_a0, iclr_a2, gate_g2, k_k, k_a, r_k, lnx_g,
              lnx_b, q_norm_g, w_uq, kv_norm_g, w_ukv, w_out, ln1_g, ln1_b, w_ffn1, w_ffn2,
              ln2_g, ln2_b)
    for l in range(depth):
        x = _layer(x, positions, *(p[l] for p in params), alpha)
    return x
```

```python
import functools
import math

import jax
import jax.numpy as jnp
from jax import lax
from jax.experimental import pallas as pl
from jax.experimental.pallas import tpu as pltpu

F32 = jnp.float32
BF16 = jnp.bfloat16

D_MODEL = 1024
RWKV_HEAD = 64
RWKV_HEADS = 8
RWKV_WIDTH = RWKV_HEADS * RWKV_HEAD
DECAY_LORA = 64
ICLR_LORA = 64
GATE_LORA = 128
MLA_HEADS = 8
MLA_NOPE = 64
MLA_ROPE = 32
MLA_V = 64
MLA_WIDTH = MLA_HEADS * MLA_V
Q_LORA = 384
KV_LORA = 256
ROPE_BASE = 10000.0
D_FF = 4 * D_MODEL
LN_EPS = 1e-5
RMS_EPS = 1e-6
GN_EPS = 64e-5
RWKV_COLS = 3 * RWKV_WIDTH + DECAY_LORA + ICLR_LORA + GATE_LORA
MLA_COLS = Q_LORA + KV_LORA + MLA_ROPE

LANES = 128
VMEM_LIMIT = 56 * 1024 * 1024

C_R, C_K, C_V = 0, RWKV_WIDTH, 2 * RWKV_WIDTH
C_WA = 3 * RWKV_WIDTH
C_G = C_WA + DECAY_LORA + ICLR_LORA
C_Q = RWKV_COLS
C_KV = C_Q + Q_LORA
C_PE = C_KV + KV_LORA
IN_COLS_PAD = C_PE + LANES
PE_LO = MLA_NOPE
PE_HALF = MLA_ROPE // 2

CHUNK = 64
ATTN_GROUP = 4
ATTN_ONES_ROWS = 16
F32_MAX_EXP2 = 126.0
ATTN_BOUND_SLACK = 1.01


def _dot(a, b):
    return jnp.dot(a.astype(BF16), b.astype(BF16), preferred_element_type=F32)


def _split_bf16(x):
    hi = x.astype(BF16)
    lo = (x - hi.astype(F32)).astype(BF16)
    return hi, lo


def _sigmoid(z):
    return 1.0 / (1.0 + jnp.exp(-z))


def _layer_norm(x, g, b):
    mu = jnp.mean(x, axis=-1, keepdims=True)
    d = x - mu
    var = jnp.mean(d * d, axis=-1, keepdims=True)
    return d * lax.rsqrt(var + LN_EPS) * g + b


def _prep_kernel(x_ref, pos_ref, w_ref, mu_ref, lora_ref, w0a0_ref, g2_ref, qg_ref, wq_ref,
                 kvg_ref, wk_ref, wvt_ref, freq_ref, sgn_ref,
                 r_ref, k_ref, v_ref, lw_ref, a_ref, g_ref, q_ref, kh_ref, vt_ref, carry_ref,
                 *, scale):
    tm = x_ref.shape[0]
    j = pl.program_id(1)

    @pl.when(j == 0)
    def _():
        carry_ref[...] = jnp.zeros_like(carry_ref)

    h = jnp.dot(x_ref[...].astype(BF16), w_ref[...], preferred_element_type=F32)

    hr = h[:, :RWKV_COLS]
    row = lax.broadcasted_iota(jnp.int32, hr.shape, 0)
    prev = jnp.where(row == 0, carry_ref[0:1, :], pltpu.roll(hr, 1, axis=0))
    carry_ref[0:1, :] = hr[tm - 1:tm, :]
    hr = hr + (prev - hr) * mu_ref[...]

    r_ref[...] = hr[:, C_R:C_R + RWKV_WIDTH]
    k_ref[...] = hr[:, C_K:C_K + RWKV_WIDTH]
    v_ref[...] = hr[:, C_V:C_V + RWKV_WIDTH]

    lane = lax.broadcasted_iota(jnp.int32, (tm, LANES), 1)
    xwa = hr[:, C_WA:C_WA + LANES]
    xwa = jnp.where(lane < DECAY_LORA, jnp.tanh(xwa), xwa)
    z = _dot(xwa, lora_ref[...]) + w0a0_ref[...]
    sg = _sigmoid(z)
    lw_ref[...] = (-math.exp(-0.5)) * sg[:, :RWKV_WIDTH]
    a_ref[...] = sg[:, RWKV_WIDTH:]
    g_ref[...] = _dot(_sigmoid(hr[:, C_G:C_G + GATE_LORA]), g2_ref[...])

    def rms(c, gain):
        return c * lax.rsqrt(jnp.mean(c * c, axis=-1, keepdims=True) + RMS_EPS) * gain

    cqn = rms(h[:, C_Q:C_Q + Q_LORA], qg_ref[...]).astype(BF16)
    ckvn = rms(h[:, C_KV:C_KV + KV_LORA], kvg_ref[...]).astype(BF16)
    q_all = jnp.dot(cqn, wq_ref[...], preferred_element_type=F32)
    k_all = jnp.dot(ckvn, wk_ref[...], preferred_element_type=F32)
    vt = lax.dot_general(wvt_ref[...], ckvn, (((1,), (1,)), ((), ())),
                         preferred_element_type=F32)
    vt_ref[...] = vt.astype(vt_ref.dtype)

    ang = pos_ref[...].astype(F32) * freq_ref[...]
    cosf = jnp.cos(ang)
    sins = jnp.sin(ang) * sgn_ref[...]
    first_half = (lane >= PE_LO) & (lane < PE_LO + PE_HALF)

    def rope(zg):
        rot = jnp.where(first_half, pltpu.roll(zg, LANES - PE_HALF, axis=1),
                        pltpu.roll(zg, PE_HALF, axis=1))
        return zg * cosf + rot * sins

    kpe = rope(h[:, C_PE:C_PE + LANES])
    for hd in range(MLA_HEADS):
        sl = slice(hd * LANES, (hd + 1) * LANES)
        q_ref[hd] = (rope(q_all[:, sl]) * scale).astype(q_ref.dtype)
        kh_ref[hd] = (k_all[:, sl] + kpe).astype(kh_ref.dtype)


_NN = (((2,), (1,)), ((0,), (0,)))
_NT = (((2,), (2,)), ((0,), (0,)))
_TN = (((1,), (1,)), ((0,), (0,)))


def _rwkv_kernel(r_ref, k_ref, v_ref, lw_ref, a_ref, gprev_ref, par_ref, parprev_ref, o_ref,
                 h_ref, rq_ref, ya_ref, g2_ref, ha_ref, bonus_ref, *, chunk, nsteps):
    tb = r_ref.shape[0]
    C = chunk
    nc = tb // C
    C2 = 2 * C
    s = pl.program_id(0)

    @pl.when(s == 0)
    def _():
        for ref in (h_ref, rq_ref, ya_ref, g2_ref, ha_ref, bonus_ref):
            ref[...] = jnp.zeros_like(ref)

    first_of_sequence = (jnp.maximum(s - 1, 0) % nsteps) == 0
    scan = {"h": h_ref[...] * jnp.where(first_of_sequence, 0.0, 1.0), "ys": []}

    def scan_step():
        c = len(scan["ys"])
        if c < nc:
            hb = scan["h"].astype(BF16)
            y2 = jnp.dot(rq_ref[c], hb, preferred_element_type=F32) + ya_ref[c]
            scan["ys"].append(y2[:C] + y2[C:])
            scan["h"] = jnp.dot(g2_ref[c], hb, preferred_element_type=F32) + ha_ref[c]

    lane = lax.broadcasted_iota(jnp.int32, (1, LANES), 1)
    m0 = (lane < RWKV_HEAD).astype(F32)
    m1 = 1.0 - m0
    rr = lax.broadcasted_iota(jnp.int32, (LANES, LANES), 0)
    cc = lax.broadcasted_iota(jnp.int32, (LANES, LANES), 1)
    ones2 = ((rr < RWKV_HEAD) == (cc < RWKV_HEAD)).astype(BF16)
    eye = (rr == cc).astype(F32)

    k_k, k_a, r_k = par_ref[0:1, :], par_ref[1:2, :], par_ref[2:3, :]
    ln_g, ln_b = parprev_ref[3:4, :], parprev_ref[4:5, :]

    scan_step()
    r = r_ref[...]
    k = k_ref[...]
    v = v_ref[...]
    lw = lw_ref[...]
    a = a_ref[...]

    kk0 = k * k_k
    n2 = _dot(kk0 * kk0, ones2)
    kk = kk0 / jnp.maximum(jnp.sqrt(n2), 1e-12)
    kmod = k * (1.0 + (a - 1.0) * k_a)
    bonus = _dot(r * kmod * r_k, ones2) * v
    bb = kk * a

    to3 = lambda t: t.reshape(nc, C, LANES)
    r3, k3, v3, lw3, kk3, bb3 = to3(r), to3(kmod), to3(v), to3(lw), to3(kk), to3(bb)

    ti = lax.broadcasted_iota(jnp.int32, (nc, C, C), 1)
    si = lax.broadcasted_iota(jnp.int32, (nc, C, C), 2)
    tril = (si <= ti).astype(BF16)
    lhi, llo = _split_bf16(lw3)
    cum = (lax.dot_general(tril, lhi, _NN, preferred_element_type=F32)
           + lax.dot_general(tril, llo, _NN, preferred_element_type=F32))
    scan_step()
    cend = cum[:, C - 1:C, :]
    p_inc = jnp.exp(cum)
    p_exc = jnp.exp(cum - lw3)
    p_inv = jnp.exp(-cum)
    p_end = jnp.exp(cend - cum)
    p_all = jnp.exp(cend)

    m0b, m1b = m0.astype(BF16), m1.astype(BF16)

    def expand(t):
        tb = t.astype(BF16)
        return jnp.concatenate([tb * m0b, tb * m1b], axis=1)

    rt = r3 * p_inc
    at2 = expand(-kk3 * p_exc)
    rt2 = expand(rt)
    bt2 = expand(bb3 * p_inv)
    kt2 = expand(k3 * p_inv)
    v2 = expand(v3)
    bbar2 = expand(bb3 * p_end)
    kbar2 = expand(k3 * p_end)

    bdot = functools.partial(lax.dot_general, preferred_element_type=F32)
    lhs = jnp.concatenate([at2, rt2], axis=1)
    rhs = jnp.concatenate([bt2, kt2], axis=1)
    m_all = bdot(lhs, rhs, _NT)
    ri = lax.broadcasted_iota(jnp.int32, (4 * C, 4 * C), 0)
    ci = lax.broadcasted_iota(jnp.int32, (4 * C, 4 * C), 1)
    bottom = (ri >= C2).astype(jnp.int32)
    right = (ci >= C2).astype(jnp.int32)
    keep = (ri - C2 * bottom) - (ci - C2 * right) + bottom > 0
    m_all = jnp.where(keep[None], m_all, 0.0)
    a_ab = m_all[:, :C2, :C2]
    a_ak = m_all[:, :C2, C2:].astype(BF16)
    a_r = m_all[:, C2:, :].astype(BF16)
    scan_step()

    rounds = max(1, (C - 1).bit_length()) - 1
    tmat = eye[None] + a_ab
    pw = a_ab.astype(BF16)
    pw = bdot(pw, pw, _NN).astype(BF16)
    scan_step()
    for rnd in range(rounds):
        if rnd + 1 < rounds:
            both = bdot(pw, jnp.concatenate([tmat.astype(BF16), pw], axis=2), _NN)
            tmat = tmat + both[:, :, :C2]
            pw = both[:, :, C2:].astype(BF16)
        else:
            tmat = tmat + bdot(pw, tmat.astype(BF16), _NN)
        scan_step()

    akv = bdot(a_ak, v2, _NN).astype(BF16)
    scan_step()
    wu = bdot(tmat.astype(BF16), jnp.concatenate([at2, akv], axis=2), _NN)
    wu = wu.astype(BF16)
    while len(scan["ys"]) < nc:
        scan_step()
    h_ref[...] = scan["h"]

    y = jnp.concatenate(scan["ys"], axis=0)
    inv_n = 1.0 / RWKV_HEAD
    mu = _dot(y, ones2) * inv_n
    d = y - mu
    var = _dot(d * d, ones2) * inv_n
    yn = d * lax.rsqrt(var + GN_EPS) * ln_g + ln_b
    o_ref[...] = ((yn + bonus_ref[...]) * gprev_ref[...]).astype(o_ref.dtype)

    zmat = jnp.concatenate(
        [wu, jnp.concatenate([jnp.zeros_like(v2), v2], axis=2)], axis=1)
    ry = bdot(a_r, zmat, _NN)
    bk = jnp.concatenate([bbar2, kbar2], axis=1)
    gh = bdot(bk, zmat, _TN)

    rq_ref[...] = (jnp.concatenate([rt * m0, rt * m1], axis=1) + ry[:, :, :LANES]).astype(BF16)
    ya_ref[...] = ry[:, :, LANES:]
    g2_ref[...] = (gh[:, :, :LANES] + eye[None] * p_all).astype(BF16)
    ha_ref[...] = gh[:, :, LANES:]
    bonus_ref[...] = bonus


def _attn_kernel(q_ref, k_ref, vt_ref, o_ref, fast_ref, st_ref, acc_ref):
    nheads, seq, _ = q_ref.shape
    t = vt_ref.shape[2]
    i = pl.program_id(2)
    dn = (((1,), (1,)), ((), ()))
    sub = 8

    @pl.when(i == 0)
    def _():
        worst = []
        for hd in range(nheads):
            ka = jnp.abs(k_ref[hd].astype(F32)).reshape(seq // sub, sub, LANES)
            kmax = jnp.max(ka, axis=0).astype(BF16)
            bound = lax.dot_general(kmax, jnp.abs(q_ref[hd]), dn, preferred_element_type=F32)
            worst.append(jnp.max(bound) * ATTN_BOUND_SLACK)
        vmax = jnp.max(jnp.abs(vt_ref[...].astype(F32)))
        vlog = jnp.max(jnp.log2(jnp.maximum(jnp.full((sub, LANES), vmax), 1.0)))
        budget = F32_MAX_EXP2 - math.log2(seq)
        fast_ref[0] = (functools.reduce(jnp.maximum, worst) + vlog <= budget).astype(jnp.int32)

    fast_ok = fast_ref[0] == 1

    def scores(hd, jb):
        kb = k_ref[hd, pl.ds(pl.multiple_of(jb * t, t), t), :]
        qb = q_ref[hd, pl.ds(pl.multiple_of(i * t, t), t), :]
        return lax.dot_general(kb, qb, dn, preferred_element_type=F32)

    def causal(x):
        krow = lax.broadcasted_iota(jnp.int32, x.shape, 0)
        qcol = lax.broadcasted_iota(jnp.int32, x.shape, 1)
        return jnp.where(krow <= qcol, x, -jnp.inf)

    @pl.when(fast_ok)
    def _():
        ones_v = jnp.ones((ATTN_ONES_ROWS, t), BF16)
        acc_ref[...] = jnp.zeros_like(acc_ref)

        def produce(jb, slot):
            for hd in range(nheads):
                st_ref[slot, hd] = scores(hd, jb)

        def consume(jb, slot, masked):
            for hd in range(nheads):
                x = st_ref[slot, hd]
                p = jnp.exp2(causal(x) if masked else x).astype(BF16)
                vaug = jnp.concatenate(
                    [vt_ref[jb, hd * MLA_V:(hd + 1) * MLA_V, :], ones_v], axis=0)
                acc_ref[hd] += jnp.dot(vaug, p, preferred_element_type=F32)

        produce(0, 0)

        def pair(jj, c):
            j0 = 2 * jj
            produce(j0 + 1, 1)
            consume(j0, 0, False)
            produce(j0 + 2, 0)
            consume(j0 + 1, 1, False)
            return c

        lax.fori_loop(0, i // 2, pair, 0)

        @pl.when(i % 2 == 1)
        def _():
            produce(i, 1)
            consume(i - 1, 0, False)
            consume(i, 1, True)

        @pl.when(i % 2 == 0)
        def _():
            consume(i, 0, True)

        for hd in range(nheads):
            acc = acc_ref[hd]
            o_ref[hd * MLA_V:(hd + 1) * MLA_V, :] = (
                acc[:MLA_V] / acc[MLA_V:MLA_V + 1]).astype(o_ref.dtype)

    @pl.when(jnp.logical_not(fast_ok))
    def _():
        def block(jb, carry, masked):
            out = []
            for hd in range(nheads):
                m, l, acc = carry[hd]
                st = scores(hd, jb)
                if masked:
                    st = causal(st)
                mn = jnp.maximum(m, jnp.max(st, axis=0, keepdims=True))
                alpha = jnp.exp2(m - mn)
                p = jnp.exp2(st - mn)
                l = l * alpha + jnp.sum(p, axis=0, keepdims=True)
                vb = vt_ref[jb, hd * MLA_V:(hd + 1) * MLA_V, :]
                acc = acc * alpha + jnp.dot(vb, p.astype(BF16), preferred_element_type=F32)
                out.append((mn, l, acc))
            return tuple(out)

        init = tuple((jnp.full((1, t), -jnp.inf, F32), jnp.zeros((1, t), F32),
                      jnp.zeros((MLA_V, t), F32)) for _ in range(nheads))
        carry = lax.fori_loop(0, i, lambda jb, c: block(jb, c, False), init)
        carry = block(i, carry, True)
        for hd in range(nheads):
            _, l, acc = carry[hd]
            o_ref[hd * MLA_V:(hd + 1) * MLA_V, :] = (acc / l).astype(o_ref.dtype)


def _out_kernel(x_ref, yr_ref, ymt_ref, wor_ref, wom_ref, ln1_ref, w1_ref, w2_ref, ln2_ref, o_ref,
                *, alpha, ff_chunk):
    x = x_ref[...]
    mix = jnp.dot(yr_ref[...], wor_ref[...], preferred_element_type=F32)
    mix = mix + lax.dot_general(ymt_ref[...], wom_ref[...], (((0,), (0,)), ((), ())),
                                preferred_element_type=F32)
    x1 = _layer_norm(alpha * x + mix, ln1_ref[0:1, :], ln1_ref[1:2, :])
    x1b = x1.astype(BF16)
    f = jnp.zeros_like(x1)
    for c in range(0, w1_ref.shape[1], ff_chunk):
        hid = jnp.dot(x1b, w1_ref[:, c:c + ff_chunk], preferred_element_type=F32)
        hid = jnp.square(jnp.maximum(hid, 0.0)).astype(BF16)
        f = f + jnp.dot(hid, w2_ref[c:c + ff_chunk, :], preferred_element_type=F32)
    o_ref[...] = _layer_norm(alpha * x1 + f, ln2_ref[0:1, :], ln2_ref[1:2, :])


def _const_spec(shape):
    nd = len(shape)
    return pl.BlockSpec(shape, lambda *_: (0,) * nd, pipeline_mode=pl.Buffered(1))


def _pick(n, pref):
    t = min(pref, n)
    while n % t:
        t //= 2
    return t


def _layer(x, positions, w_in, shift_mu, decay_w0, decay_w2, iclr_a0, iclr_a2, gate_g2, k_k, k_a,
           r_k, lnx_g, lnx_b, q_norm_g, w_uq, kv_norm_g, w_ukv, w_out, ln1_g, ln1_b, w_ffn1,
           w_ffn2, ln2_g, ln2_b, alpha):
    B, S, D = x.shape
    T = B * S
    H = MLA_HEADS
    tm = _pick(S, 512)
    tb = _pick(S, 512)
    chunk = _pick(tb, CHUNK)
    nt = S // tm

    w_r = w_in[:, :RWKV_COLS]
    w_m = w_in[:, RWKV_COLS:]
    w_pe = jnp.zeros((D, LANES), F32).at[:, PE_LO:PE_LO + MLA_ROPE].set(w_m[:, Q_LORA + KV_LORA:])
    w_all = jnp.concatenate([w_r, w_m[:, :Q_LORA + KV_LORA], w_pe], axis=1).astype(BF16)
    mu = shift_mu.reshape(1, RWKV_COLS)
    lora = jnp.zeros((LANES, 2 * RWKV_WIDTH), F32)
    lora = lora.at[:DECAY_LORA, :RWKV_WIDTH].set(decay_w2).at[DECAY_LORA:, RWKV_WIDTH:].set(iclr_a2)
    lora = lora.astype(BF16)
    w0a0 = jnp.concatenate([decay_w0, iclr_a0]).reshape(1, 2 * RWKV_WIDTH)
    g2 = gate_g2.astype(BF16)
    dq = MLA_NOPE + MLA_ROPE
    wq = jnp.zeros((Q_LORA, H, LANES), F32).at[:, :, :dq].set(w_uq.reshape(Q_LORA, H, dq))
    wq = wq.reshape(Q_LORA, H * LANES).astype(BF16)
    wkv = w_ukv.reshape(KV_LORA, H, MLA_NOPE + MLA_V)
    wk = jnp.zeros((KV_LORA, H, LANES), F32).at[:, :, :MLA_NOPE].set(wkv[:, :, :MLA_NOPE])
    wk = wk.reshape(KV_LORA, H * LANES).astype(BF16)
    wvt = wkv[:, :, MLA_NOPE:].reshape(KV_LORA, H * MLA_V).T.astype(BF16)
    inv_freq = ROPE_BASE ** (-jnp.arange(0, MLA_ROPE, 2, dtype=F32) / MLA_ROPE)
    freq = jnp.zeros((1, LANES), F32).at[0, PE_LO:PE_LO + PE_HALF].set(inv_freq)
    freq = freq.at[0, PE_LO + PE_HALF:PE_LO + MLA_ROPE].set(inv_freq)
    sgn = jnp.zeros((1, LANES), F32).at[0, PE_LO:PE_LO + PE_HALF].set(-1.0)
    sgn = sgn.at[0, PE_LO + PE_HALF:PE_LO + MLA_ROPE].set(1.0)
    par = jnp.zeros((8, RWKV_WIDTH), F32)
    par = par.at[0].set(k_k).at[1].set(k_a).at[2].set(r_k.reshape(-1)).at[3].set(lnx_g).at[4].set(lnx_b)
    ln1 = jnp.stack([ln1_g, ln1_b])
    ln2 = jnp.stack([ln2_g, ln2_b])
    wo_r = w_out[:RWKV_WIDTH].astype(BF16)
    wo_m = w_out[RWKV_WIDTH:].astype(BF16)
    w1 = w_ffn1.astype(BF16)
    w2 = w_ffn2.astype(BF16)

    pos3 = positions.reshape(B, S, 1)

    tok = lambda: pl.BlockSpec((None, tm, RWKV_WIDTH), lambda b, j: (b, j, 0))
    tok_shape = jax.ShapeDtypeStruct((B, S, RWKV_WIDTH), F32)
    head_spec = pl.BlockSpec((None, H, tm, LANES), lambda b, j: (b, 0, j, 0))
    r, k, v, lw, a, g, q, kh, vt = pl.pallas_call(
        functools.partial(_prep_kernel, scale=float(dq) ** -0.5 * math.log2(math.e)),
        grid=(B, nt),
        in_specs=[
            pl.BlockSpec((None, tm, D), lambda b, j: (b, j, 0)),
            pl.BlockSpec((None, tm, 1), lambda b, j: (b, j, 0)),
            _const_spec(w_all.shape), _const_spec(mu.shape), _const_spec(lora.shape),
            _const_spec(w0a0.shape), _const_spec(g2.shape),
            _const_spec((1, Q_LORA)), _const_spec(wq.shape),
            _const_spec((1, KV_LORA)), _const_spec(wk.shape), _const_spec(wvt.shape),
            _const_spec(freq.shape), _const_spec(sgn.shape),
        ],
        out_specs=[tok(), tok(), tok(), tok(), tok(), tok(), head_spec, head_spec,
                   pl.BlockSpec((None, None, H * MLA_V, tm), lambda b, j: (b, j, 0, 0))],
        out_shape=[tok_shape] * 6 + [
            jax.ShapeDtypeStruct((B, H, S, LANES), BF16),
            jax.ShapeDtypeStruct((B, H, S, LANES), BF16),
            jax.ShapeDtypeStruct((B, nt, H * MLA_V, tm), BF16)],
        scratch_shapes=[pltpu.VMEM((8, RWKV_COLS), F32)],
        compiler_params=pltpu.CompilerParams(
            dimension_semantics=("arbitrary", "arbitrary"), vmem_limit_bytes=VMEM_LIMIT),
    )(x, pos3, w_all, mu, lora, w0a0, g2, q_norm_g.reshape(1, Q_LORA), wq,
      kv_norm_g.reshape(1, KV_LORA), wk, wvt, freq, sgn)

    npair = RWKV_WIDTH // LANES
    nsteps = S // tb
    nblocks = B * npair * nsteps
    nc = tb // chunk

    def block_of(s):
        return s // (npair * nsteps), s % nsteps, (s // nsteps) % npair

    cur = lambda: pl.BlockSpec((None, tb, LANES), lambda s: block_of(jnp.minimum(s, nblocks - 1)))
    prev = lambda: pl.BlockSpec((None, tb, LANES), lambda s: block_of(jnp.maximum(s - 1, 0)))
    y_rwkv = pl.pallas_call(
        functools.partial(_rwkv_kernel, chunk=chunk, nsteps=nsteps),
        grid=(nblocks + 1,),
        in_specs=[cur(), cur(), cur(), cur(), cur(), prev(),
                  pl.BlockSpec((8, LANES), lambda s: (0, block_of(jnp.minimum(s, nblocks - 1))[2])),
                  pl.BlockSpec((8, LANES), lambda s: (0, block_of(jnp.maximum(s - 1, 0))[2]))],
        out_specs=prev(),
        out_shape=jax.ShapeDtypeStruct((B, S, RWKV_WIDTH), BF16),
        scratch_shapes=[pltpu.VMEM((LANES, LANES), F32),
                        pltpu.VMEM((nc, 2 * chunk, LANES), BF16),
                        pltpu.VMEM((nc, 2 * chunk, LANES), F32),
                        pltpu.VMEM((nc, LANES, LANES), BF16),
                        pltpu.VMEM((nc, LANES, LANES), F32),
                        pltpu.VMEM((tb, LANES), F32)],
        compiler_params=pltpu.CompilerParams(
            dimension_semantics=("arbitrary",), vmem_limit_bytes=VMEM_LIMIT),
    )(r, k, v, lw, a, g, par, par)

    y_mla_t = pl.pallas_call(
        _attn_kernel,
        grid=(B, H // ATTN_GROUP, nt),
        in_specs=[
            pl.BlockSpec((None, ATTN_GROUP, S, LANES), lambda b, h, i: (b, h, 0, 0)),
            pl.BlockSpec((None, ATTN_GROUP, S, LANES), lambda b, h, i: (b, h, 0, 0)),
            pl.BlockSpec((None, nt, ATTN_GROUP * MLA_V, tm), lambda b, h, i: (b, 0, h, 0)),
        ],
        out_specs=pl.BlockSpec((None, ATTN_GROUP * MLA_V, tm), lambda b, h, i: (b, h, i)),
        out_shape=jax.ShapeDtypeStruct((B, H * MLA_V, S), BF16),
        scratch_shapes=[pltpu.SMEM((1,), jnp.int32),
                        pltpu.VMEM((2, ATTN_GROUP, tm, tm), F32),
                        pltpu.VMEM((ATTN_GROUP, MLA_V + ATTN_ONES_ROWS, tm), F32)],
        compiler_params=pltpu.CompilerParams(
            dimension_semantics=("arbitrary", "arbitrary", "arbitrary"),
            vmem_limit_bytes=VMEM_LIMIT),
    )(q, kh, vt)

    out = pl.pallas_call(
        functools.partial(_out_kernel, alpha=alpha, ff_chunk=_pick(D_FF, 1024)),
        grid=(B, nt),
        in_specs=[
            pl.BlockSpec((None, tm, D), lambda b, j: (b, j, 0)),
            pl.BlockSpec((None, tm, RWKV_WIDTH), lambda b, j: (b, j, 0)),
            pl.BlockSpec((None, MLA_WIDTH, tm), lambda b, j: (b, 0, j)),
            _const_spec(wo_r.shape), _const_spec(wo_m.shape), _const_spec(ln1.shape),
            _const_spec(w1.shape), _const_spec(w2.shape), _const_spec(ln2.shape),
        ],
        out_specs=pl.BlockSpec((None, tm, D), lambda b, j: (b, j, 0)),
        out_shape=jax.ShapeDtypeStruct((B, S, D), F32),
        compiler_params=pltpu.CompilerParams(
            dimension_semantics=("arbitrary", "arbitrary"), vmem_limit_bytes=VMEM_LIMIT),
    )(x, y_rwkv, y_mla_t, wo_r, wo_m, ln1, w1, w2, ln2)
    return out


def kernel(x, positions, w_in, shift_mu, decay_w0, decay_w2, iclr_a0, iclr_a2, gate_g2, k_k, k_a, r_k, lnx_g, lnx_b, q_norm_g, w_uq, kv_norm_g, w_ukv, w_out, ln1_g, ln1_b, w_ffn1, w_ffn2, ln2_g, ln2_b):
    depth = w_in.shape[0]
    alpha = (2 * depth) ** 0.25
    params = (w_in, shift_mu, decay_w0, decay_w2, iclr_a0, iclr_a2, gate_g2, k_k, k_a, r_k, lnx_g,
              lnx_b, q_norm_g, w_uq, kv_norm_g, w_ukv, w_out, ln1_g, ln1_b, w_ffn1, w_ffn2,
              ln2_g, ln2_b)
    for l in range(depth):
        x = _layer(x, positions, *(p[l] for p in params), alpha)
    return x
```

```python
import functools
import math

import jax
import jax.numpy as jnp
from jax import lax
from jax.experimental import pallas as pl
from jax.experimental.pallas import tpu as pltpu

F32 = jnp.float32
BF16 = jnp.bfloat16

D_MODEL = 1024
RWKV_HEAD = 64
RWKV_HEADS = 8
RWKV_WIDTH = RWKV_HEADS * RWKV_HEAD
DECAY_LORA = 64
ICLR_LORA = 64
GATE_LORA = 128
MLA_HEADS = 8
MLA_NOPE = 64
MLA_ROPE = 32
MLA_V = 64
MLA_WIDTH = MLA_HEADS * MLA_V
Q_LORA = 384
KV_LORA = 256
ROPE_BASE = 10000.0
D_FF = 4 * D_MODEL
LN_EPS = 1e-5
RMS_EPS = 1e-6
GN_EPS = 64e-5
RWKV_COLS = 3 * RWKV_WIDTH + DECAY_LORA + ICLR_LORA + GATE_LORA
MLA_COLS = Q_LORA + KV_LORA + MLA_ROPE

LANES = 128
VMEM_LIMIT = 56 * 1024 * 1024

C_R, C_K, C_V = 0, RWKV_WIDTH, 2 * RWKV_WIDTH
C_WA = 3 * RWKV_WIDTH
C_G = C_WA + DECAY_LORA + ICLR_LORA
C_Q = RWKV_COLS
C_KV = C_Q + Q_LORA
C_PE = C_KV + KV_LORA
IN_COLS_PAD = C_PE + LANES
PE_LO = MLA_NOPE
PE_HALF = MLA_ROPE // 2

CHUNK = 64
ATTN_GROUP = 4
ATTN_ONES_ROWS = 16
F32_MAX_EXP2 = 126.0
ATTN_BOUND_SLACK = 1.01


def _dot(a, b):
    return jnp.dot(a.astype(BF16), b.astype(BF16), preferred_element_type=F32)


def _split_bf16(x):
    hi = x.astype(BF16)
    lo = (x - hi.astype(F32)).astype(BF16)
    return hi, lo


def _sigmoid(z):
    return 1.0 / (1.0 + jnp.exp(-z))


def _layer_norm(x, g, b):
    mu = jnp.mean(x, axis=-1, keepdims=True)
    d = x - mu
    var = jnp.mean(d * d, axis=-1, keepdims=True)
    return d * lax.rsqrt(var + LN_EPS) * g + b


def _prep_kernel(x_ref, pos_ref, w_ref, mu_ref, lora_ref, w0a0_ref, g2_ref, qg_ref, wq_ref,
                 kvg_ref, wk_ref, wvt_ref, freq_ref, sgn_ref,
                 r_ref, k_ref, v_ref, lw_ref, a_ref, g_ref, q_ref, kh_ref, vt_ref, carry_ref,
                 *, scale):
    tm = x_ref.shape[0]
    j = pl.program_id(1)

    @pl.when(j == 0)
    def _():
        carry_ref[...] = jnp.zeros_like(carry_ref)

    h = jnp.dot(x_ref[...].astype(BF16), w_ref[...], preferred_element_type=F32)

    hr = h[:, :RWKV_COLS]
    row = lax.broadcasted_iota(jnp.int32, hr.shape, 0)
    prev = jnp.where(row == 0, carry_ref[0:1, :], pltpu.roll(hr, 1, axis=0))
    carry_ref[0:1, :] = hr[tm - 1:tm, :]
    hr = hr + (prev - hr) * mu_ref[...]

    r_ref[...] = hr[:, C_R:C_R + RWKV_WIDTH]
    k_ref[...] = hr[:, C_K:C_K + RWKV_WIDTH]
    v_ref[...] = hr[:, C_V:C_V + RWKV_WIDTH]

    lane = lax.broadcasted_iota(jnp.int32, (tm, LANES), 1)
    xwa = hr[:, C_WA:C_WA + LANES]
    xwa = jnp.where(lane < DECAY_LORA, jnp.tanh(xwa), xwa)
    z = _dot(xwa, lora_ref[...]) + w0a0_ref[...]
    sg = _sigmoid(z)
    lw_ref[...] = (-math.exp(-0.5)) * sg[:, :RWKV_WIDTH]
    a_ref[...] = sg[:, RWKV_WIDTH:]
    g_ref[...] = _dot(_sigmoid(hr[:, C_G:C_G + GATE_LORA]), g2_ref[...])

    def rms(c, gain):
        return c * lax.rsqrt(jnp.mean(c * c, axis=-1, keepdims=True) + RMS_EPS) * gain

    cqn = rms(h[:, C_Q:C_Q + Q_LORA], qg_ref[...]).astype(BF16)
    ckvn = rms(h[:, C_KV:C_KV + KV_LORA], kvg_ref[...]).astype(BF16)
    q_all = jnp.dot(cqn, wq_ref[...], preferred_element_type=F32)
    k_all = jnp.dot(ckvn, wk_ref[...], preferred_element_type=F32)
    vt = lax.dot_general(wvt_ref[...], ckvn, (((1,), (1,)), ((), ())),
                         preferred_element_type=F32)
    vt_ref[...] = vt.astype(vt_ref.dtype)

    ang = pos_ref[...].astype(F32) * freq_ref[...]
    cosf = jnp.cos(ang)
    sins = jnp.sin(ang) * sgn_ref[...]
    first_half = (lane >= PE_LO) & (lane < PE_LO + PE_HALF)

    def rope(zg):
        rot = jnp.where(first_half, pltpu.roll(zg, LANES - PE_HALF, axis=1),
                        pltpu.roll(zg, PE_HALF, axis=1))
        return zg * cosf + rot * sins

    kpe = rope(h[:, C_PE:C_PE + LANES])
    for hd in range(MLA_HEADS):
        sl = slice(hd * LANES, (hd + 1) * LANES)
        q_ref[hd] = (rope(q_all[:, sl]) * scale).astype(q_ref.dtype)
        kh_ref[hd] = (k_all[:, sl] + kpe).astype(kh_ref.dtype)


_NN = (((2,), (1,)), ((0,), (0,)))
_NT = (((2,), (2,)), ((0,), (0,)))
_TN = (((1,), (1,)), ((0,), (0,)))


def _rwkv_kernel(r_ref, k_ref, v_ref, lw_ref, a_ref, gprev_ref, par_ref, parprev_ref, o_ref,
                 h_ref, rq_ref, ya_ref, g2_ref, ha_ref, bonus_ref, *, chunk, nsteps):
    tb = r_ref.shape[0]
    C = chunk
    nc = tb // C
    C2 = 2 * C
    s = pl.program_id(0)

    @pl.when(s == 0)
    def _():
        for ref in (h_ref, rq_ref, ya_ref, g2_ref, ha_ref, bonus_ref):
            ref[...] = jnp.zeros_like(ref)

    first_of_sequence = (jnp.maximum(s - 1, 0) % nsteps) == 0
    scan = {"h": h_ref[...] * jnp.where(first_of_sequence, 0.0, 1.0), "ys": []}

    def scan_step():
        c = len(scan["ys"])
        if c < nc:
            hb = scan["h"].astype(BF16)
            y2 = jnp.dot(rq_ref[c], hb, preferred_element_type=F32) + ya_ref[c]
            scan["ys"].append(y2[:C] + y2[C:])
            scan["h"] = jnp.dot(g2_ref[c], hb, preferred_element_type=F32) + ha_ref[c]

    lane = lax.broadcasted_iota(jnp.int32, (1, LANES), 1)
    m0 = (lane < RWKV_HEAD).astype(F32)
    m1 = 1.0 - m0
    rr = lax.broadcasted_iota(jnp.int32, (LANES, LANES), 0)
    cc = lax.broadcasted_iota(jnp.int32, (LANES, LANES), 1)
    ones2 = ((rr < RWKV_HEAD) == (cc < RWKV_HEAD)).astype(BF16)
    eye = (rr == cc).astype(F32)

    k_k, k_a, r_k = par_ref[0:1, :], par_ref[1:2, :], par_ref[2:3, :]
    ln_g, ln_b = parprev_ref[3:4, :], parprev_ref[4:5, :]

    scan_step()
    r = r_ref[...]
    k = k_ref[...]
    v = v_ref[...]
    lw = lw_ref[...]
    a = a_ref[...]

    kk0 = k * k_k
    n2 = _dot(kk0 * kk0, ones2)
    kk = kk0 / jnp.maximum(jnp.sqrt(n2), 1e-12)
    kmod = k * (1.0 + (a - 1.0) * k_a)
    bonus = _dot(r * kmod * r_k, ones2) * v
    bb = kk * a

    to3 = lambda t: t.reshape(nc, C, LANES)
    r3, k3, v3, lw3, kk3, bb3 = to3(r), to3(kmod), to3(v), to3(lw), to3(kk), to3(bb)

    ti = lax.broadcasted_iota(jnp.int32, (nc, C, C), 1)
    si = lax.broadcasted_iota(jnp.int32, (nc, C, C), 2)
    tril = (si <= ti).astype(BF16)
    lhi, llo = _split_bf16(lw3)
    cum = (lax.dot_general(tril, lhi, _NN, preferred_element_type=F32)
           + lax.dot_general(tril, llo, _NN, preferred_element_type=F32))
    scan_step()
    cend = cum[:, C - 1:C, :]
    p_inc = jnp.exp(cum)
    p_exc = jnp.exp(cum - lw3)
    p_inv = jnp.exp(-cum)
    p_end = jnp.exp(cend - cum)
    p_all = jnp.exp(cend)

    m0b, m1b = m0.astype(BF16), m1.astype(BF16)

    def expand(t):
        tb = t.astype(BF16)
        return jnp.concatenate([tb * m0b, tb * m1b], axis=1)

    rt = r3 * p_inc
    at2 = expand(-kk3 * p_exc)
    rt2 = expand(rt)
    bt2 = expand(bb3 * p_inv)
    kt2 = expand(k3 * p_inv)
    v2 = expand(v3)
    bbar2 = expand(bb3 * p_end)
    kbar2 = expand(k3 * p_end)

    bdot = functools.partial(lax.dot_general, preferred_element_type=F32)
    lhs = jnp.concatenate([at2, rt2], axis=1)
    rhs = jnp.concatenate([bt2, kt2], axis=1)
    m_all = bdot(lhs, rhs, _NT)
    ri = lax.broadcasted_iota(jnp.int32, (4 * C, 4 * C), 0)
    ci = lax.broadcasted_iota(jnp.int32, (4 * C, 4 * C), 1)
    bottom = (ri >= C2).astype(jnp.int32)
    right = (ci >= C2).astype(jnp.int32)
    keep = (ri - C2 * bottom) - (ci - C2 * right) + bottom > 0
    m_all = jnp.where(keep[None], m_all, 0.0)
    a_ab = m_all[:, :C2, :C2]
    a_ak = m_all[:, :C2, C2:].astype(BF16)
    a_r = m_all[:, C2:, :].astype(BF16)
    scan_step()

    rounds = max(1, (C - 1).bit_length()) - 1
    tmat = eye[None] + a_ab
    pw = a_ab.astype(BF16)
    pw = bdot(pw, pw, _NN).astype(BF16)
    scan_step()
    for rnd in range(rounds):
        if rnd + 1 < rounds:
            both = bdot(pw, jnp.concatenate([tmat.astype(BF16), pw], axis=2), _NN)
            tmat = tmat + both[:, :, :C2]
            pw = both[:, :, C2:].astype(BF16)
        else:
            tmat = tmat + bdot(pw, tmat.astype(BF16), _NN)
        scan_step()

    akv = bdot(a_ak, v2, _NN).astype(BF16)
    scan_step()
    wu = bdot(tmat.astype(BF16), jnp.concatenate([at2, akv], axis=2), _NN)
    wu = wu.astype(BF16)
    while len(scan["ys"]) < nc:
        scan_step()
    h_ref[...] = scan["h"]

    y = jnp.concatenate(scan["ys"], axis=0)
    inv_n = 1.0 / RWKV_HEAD
    mu = _dot(y, ones2) * inv_n
    d = y - mu
    var = _dot(d * d, ones2) * inv_n
    yn = d * lax.rsqrt(var + GN_EPS) * ln_g + ln_b
    o_ref[...] = ((yn + bonus_ref[...]) * gprev_ref[...]).astype(o_ref.dtype)

    zmat = jnp.concatenate(
        [wu, jnp.concatenate([jnp.zeros_like(v2), v2], axis=2)], axis=1)
    ry = bdot(a_r, zmat, _NN)
    bk = jnp.concatenate([bbar2, kbar2], axis=1)
    gh = bdot(bk, zmat, _TN)

    rq_ref[...] = (jnp.concatenate([rt * m0, rt * m1], axis=1) + ry[:, :, :LANES]).astype(BF16)
    ya_ref[...] = ry[:, :, LANES:]
    g2_ref[...] = (gh[:, :, :LANES] + eye[None] * p_all).astype(BF16)
    ha_ref[...] = gh[:, :, LANES:]
    bonus_ref[...] = bonus


def _attn_kernel(q_ref, k_ref, vt_ref, o_ref, fast_ref, st_ref, acc_ref):
    nheads, seq, _ = q_ref.shape
    t = vt_ref.shape[2]
    i = pl.program_id(2)
    dn = (((1,), (1,)), ((), ()))
    sub = 8

    @pl.when(i == 0)
    def _():
        worst = []
        for hd in range(nheads):
            ka = jnp.abs(k_ref[hd].astype(F32)).reshape(seq // sub, sub, LANES)
            kmax = jnp.max(ka, axis=0).astype(BF16)
            bound = lax.dot_general(kmax, jnp.abs(q_ref[hd]), dn, preferred_element_type=F32)
            worst.append(jnp.max(bound) * ATTN_BOUND_SLACK)
        vmax = jnp.max(jnp.abs(vt_ref[...].astype(F32)))
        vlog = jnp.max(jnp.log2(jnp.maximum(jnp.full((sub, LANES), vmax), 1.0)))
        budget = F32_MAX_EXP2 - math.log2(seq)
        fast_ref[0] = (functools.reduce(jnp.maximum, worst) + vlog <= budget).astype(jnp.int32)

    fast_ok = fast_ref[0] == 1

    def scores(hd, jb):
        kb = k_ref[hd, pl.ds(pl.multiple_of(jb * t, t), t), :]
        qb = q_ref[hd, pl.ds(pl.multiple_of(i * t, t), t), :]
        return lax.dot_general(kb, qb, dn, preferred_element_type=F32)

    def causal(x):
        krow = lax.broadcasted_iota(jnp.int32, x.shape, 0)
        qcol = lax.broadcasted_iota(jnp.int32, x.shape, 1)
        return jnp.where(krow <= qcol, x, -jnp.inf)

    @pl.when(fast_ok)
    def _():
        ones_v = jnp.ones((ATTN_ONES_ROWS, t), BF16)
        acc_ref[...] = jnp.zeros_like(acc_ref)

        def produce(jb, slot):
            for hd in range(nheads):
                st_ref[slot, hd] = scores(hd, jb)

        def consume(jb, slot, masked):
            for hd in range(nheads):
                x = st_ref[slot, hd]
                p = jnp.exp2(causal(x) if masked else x).astype(BF16)
                vaug = jnp.concatenate(
                    [vt_ref[jb, hd * MLA_V:(hd + 1) * MLA_V, :], ones_v], axis=0)
                acc_ref[hd] += jnp.dot(vaug, p, preferred_element_type=F32)

        produce(0, 0)

        def pair(jj, c):
            j0 = 2 * jj
            produce(j0 + 1, 1)
            consume(j0, 0, False)
            produce(j0 + 2, 0)
            consume(j0 + 1, 1, False)
            return c

        lax.fori_loop(0, i // 2, pair, 0)

        @pl.when(i % 2 == 1)
        def _():
            produce(i, 1)
            consume(i - 1, 0, False)
            consume(i, 1, True)

        @pl.when(i % 2 == 0)
        def _():
            consume(i, 0, True)

        for hd in range(nheads):
            acc = acc_ref[hd]
            o_ref[hd * MLA_V:(hd + 1) * MLA_V, :] = (
                acc[:MLA_V] / acc[MLA_V:MLA_V + 1]).astype(o_ref.dtype)

    @pl.when(jnp.logical_not(fast_ok))
    def _():
        def block(jb, carry, masked):
            out = []
            for hd in range(nheads):
                m, l, acc = carry[hd]
                st = scores(hd, jb)
                if masked:
                    st = causal(st)
                mn = jnp.maximum(m, jnp.max(st, axis=0, keepdims=True))
                alpha = jnp.exp2(m - mn)
                p = jnp.exp2(st - mn)
                l = l * alpha + jnp.sum(p, axis=0, keepdims=True)
                vb = vt_ref[jb, hd * MLA_V:(hd + 1) * MLA_V, :]
                acc = acc * alpha + jnp.dot(vb, p.astype(BF16), preferred_element_type=F32)
                out.append((mn, l, acc))
            return tuple(out)

        init = tuple((jnp.full((1, t), -jnp.inf, F32), jnp.zeros((1, t), F32),
                      jnp.zeros((MLA_V, t), F32)) for _ in range(nheads))
        carry = lax.fori_loop(0, i, lambda jb, c: block(jb, c, False), init)
        carry = block(i, carry, True)
        for hd in range(nheads):
            _, l, acc = carry[hd]
            o_ref[hd * MLA_V:(hd + 1) * MLA_V, :] = (acc / l).astype(o_ref.dtype)


def _out_kernel(x_ref, yr_ref, ymt_ref, wor_ref, wom_ref, ln1_ref, w1_ref, w2_ref, ln2_ref, o_ref,
                *, alpha, ff_chunk, parts):
    rows = x_ref.shape[0] // parts
    part = [slice(p * rows, (p + 1) * rows) for p in range(parts)]
    mixes = []
    for sl in part:
        mix = jnp.dot(yr_ref[sl, :], wor_ref[...], preferred_element_type=F32)
        mixes.append(mix + lax.dot_general(ymt_ref[:, sl], wom_ref[...], (((0,), (0,)), ((), ())),
                                           preferred_element_type=F32))
    x1s = [_layer_norm(alpha * x_ref[sl, :] + mix, ln1_ref[0:1, :], ln1_ref[1:2, :])
           for sl, mix in zip(part, mixes)]
    for sl, x1 in zip(part, x1s):
        x1b = x1.astype(BF16)
        f = jnp.zeros_like(x1)
        for c in range(0, w1_ref.shape[1], ff_chunk):
            hid = jnp.dot(x1b, w1_ref[:, c:c + ff_chunk], preferred_element_type=F32)
            hid = jnp.square(jnp.maximum(hid, 0.0)).astype(BF16)
            f = f + jnp.dot(hid, w2_ref[c:c + ff_chunk, :], preferred_element_type=F32)
        o_ref[sl, :] = _layer_norm(alpha * x1 + f, ln2_ref[0:1, :], ln2_ref[1:2, :])


def _const_spec(shape):
    nd = len(shape)
    return pl.BlockSpec(shape, lambda *_: (0,) * nd, pipeline_mode=pl.Buffered(1))


def _pick(n, pref):
    t = min(pref, n)
    while n % t:
        t //= 2
    return t


def _layer(x, positions, w_in, shift_mu, decay_w0, decay_w2, iclr_a0, iclr_a2, gate_g2, k_k, k_a,
           r_k, lnx_g, lnx_b, q_norm_g, w_uq, kv_norm_g, w_ukv, w_out, ln1_g, ln1_b, w_ffn1,
           w_ffn2, ln2_g, ln2_b, alpha):
    B, S, D = x.shape
    T = B * S
    H = MLA_HEADS
    tm = _pick(S, 512)
    tb = _pick(S, 512)
    to = _pick(S, 2 * tm)
    chunk = _pick(tb, CHUNK)
    nt = S // tm

    w_r = w_in[:, :RWKV_COLS]
    w_m = w_in[:, RWKV_COLS:]
    w_pe = jnp.zeros((D, LANES), F32).at[:, PE_LO:PE_LO + MLA_ROPE].set(w_m[:, Q_LORA + KV_LORA:])
    w_all = jnp.concatenate([w_r, w_m[:, :Q_LORA + KV_LORA], w_pe], axis=1).astype(BF16)
    mu = shift_mu.reshape(1, RWKV_COLS)
    lora = jnp.zeros((LANES, 2 * RWKV_WIDTH), F32)
    lora = lora.at[:DECAY_LORA, :RWKV_WIDTH].set(decay_w2).at[DECAY_LORA:, RWKV_WIDTH:].set(iclr_a2)
    lora = lora.astype(BF16)
    w0a0 = jnp.concatenate([decay_w0, iclr_a0]).reshape(1, 2 * RWKV_WIDTH)
    g2 = gate_g2.astype(BF16)
    dq = MLA_NOPE + MLA_ROPE
    wq = jnp.zeros((Q_LORA, H, LANES), F32).at[:, :, :dq].set(w_uq.reshape(Q_LORA, H, dq))
    wq = wq.reshape(Q_LORA, H * LANES).astype(BF16)
    wkv = w_ukv.reshape(KV_LORA, H, MLA_NOPE + MLA_V)
    wk = jnp.zeros((KV_LORA, H, LANES), F32).at[:, :, :MLA_NOPE].set(wkv[:, :, :MLA_NOPE])
    wk = wk.reshape(KV_LORA, H * LANES).astype(BF16)
    wvt = wkv[:, :, MLA_NOPE:].reshape(KV_LORA, H * MLA_V).T.astype(BF16)
    inv_freq = ROPE_BASE ** (-jnp.arange(0, MLA_ROPE, 2, dtype=F32) / MLA_ROPE)
    freq = jnp.zeros((1, LANES), F32).at[0, PE_LO:PE_LO + PE_HALF].set(inv_freq)
    freq = freq.at[0, PE_LO + PE_HALF:PE_LO + MLA_ROPE].set(inv_freq)
    sgn = jnp.zeros((1, LANES), F32).at[0, PE_LO:PE_LO + PE_HALF].set(-1.0)
    sgn = sgn.at[0, PE_LO + PE_HALF:PE_LO + MLA_ROPE].set(1.0)
    par = jnp.zeros((8, RWKV_WIDTH), F32)
    par = par.at[0].set(k_k).at[1].set(k_a).at[2].set(r_k.reshape(-1)).at[3].set(lnx_g).at[4].set(lnx_b)
    ln1 = jnp.stack([ln1_g, ln1_b])
    ln2 = jnp.stack([ln2_g, ln2_b])
    wo_r = w_out[:RWKV_WIDTH].astype(BF16)
    wo_m = w_out[RWKV_WIDTH:].astype(BF16)
    w1 = w_ffn1.astype(BF16)
    w2 = w_ffn2.astype(BF16)

    pos3 = positions.reshape(B, S, 1)

    tok = lambda: pl.BlockSpec((None, tm, RWKV_WIDTH), lambda b, j: (b, j, 0))
    tok_shape = jax.ShapeDtypeStruct((B, S, RWKV_WIDTH), F32)
    head_spec = pl.BlockSpec((None, H, tm, LANES), lambda b, j: (b, 0, j, 0))
    r, k, v, lw, a, g, q, kh, vt = pl.pallas_call(
        functools.partial(_prep_kernel, scale=float(dq) ** -0.5 * math.log2(math.e)),
        grid=(B, nt),
        in_specs=[
            pl.BlockSpec((None, tm, D), lambda b, j: (b, j, 0)),
            pl.BlockSpec((None, tm, 1), lambda b, j: (b, j, 0)),
            _const_spec(w_all.shape), _const_spec(mu.shape), _const_spec(lora.shape),
            _const_spec(w0a0.shape), _const_spec(g2.shape),
            _const_spec((1, Q_LORA)), _const_spec(wq.shape),
            _const_spec((1, KV_LORA)), _const_spec(wk.shape), _const_spec(wvt.shape),
            _const_spec(freq.shape), _const_spec(sgn.shape),
        ],
        out_specs=[tok(), tok(), tok(), tok(), tok(), tok(), head_spec, head_spec,
                   pl.BlockSpec((None, None, H * MLA_V, tm), lambda b, j: (b, j, 0, 0))],
        out_shape=[tok_shape] * 6 + [
            jax.ShapeDtypeStruct((B, H, S, LANES), BF16),
            jax.ShapeDtypeStruct((B, H, S, LANES), BF16),
            jax.ShapeDtypeStruct((B, nt, H * MLA_V, tm), BF16)],
        scratch_shapes=[pltpu.VMEM((8, RWKV_COLS), F32)],
        compiler_params=pltpu.CompilerParams(
            dimension_semantics=("arbitrary", "arbitrary"), vmem_limit_bytes=VMEM_LIMIT),
    )(x, pos3, w_all, mu, lora, w0a0, g2, q_norm_g.reshape(1, Q_LORA), wq,
      kv_norm_g.reshape(1, KV_LORA), wk, wvt, freq, sgn)

    npair = RWKV_WIDTH // LANES
    nsteps = S // tb
    nblocks = B * npair * nsteps
    nc = tb // chunk

    def block_of(s):
        return s // (npair * nsteps), s % nsteps, (s // nsteps) % npair

    cur = lambda: pl.BlockSpec((None, tb, LANES), lambda s: block_of(jnp.minimum(s, nblocks - 1)))
    prev = lambda: pl.BlockSpec((None, tb, LANES), lambda s: block_of(jnp.maximum(s - 1, 0)))
    y_rwkv = pl.pallas_call(
        functools.partial(_rwkv_kernel, chunk=chunk, nsteps=nsteps),
        grid=(nblocks + 1,),
        in_specs=[cur(), cur(), cur(), cur(), cur(), prev(),
                  pl.BlockSpec((8, LANES), lambda s: (0, block_of(jnp.minimum(s, nblocks - 1))[2])),
                  pl.BlockSpec((8, LANES), lambda s: (0, block_of(jnp.maximum(s - 1, 0))[2]))],
        out_specs=prev(),
        out_shape=jax.ShapeDtypeStruct((B, S, RWKV_WIDTH), BF16),
        scratch_shapes=[pltpu.VMEM((LANES, LANES), F32),
                        pltpu.VMEM((nc, 2 * chunk, LANES), BF16),
                        pltpu.VMEM((nc, 2 * chunk, LANES), F32),
                        pltpu.VMEM((nc, LANES, LANES), BF16),
                        pltpu.VMEM((nc, LANES, LANES), F32),
                        pltpu.VMEM((tb, LANES), F32)],
        compiler_params=pltpu.CompilerParams(
            dimension_semantics=("arbitrary",), vmem_limit_bytes=VMEM_LIMIT),
    )(r, k, v, lw, a, g, par, par)

    y_mla_t = pl.pallas_call(
        _attn_kernel,
        grid=(B, H // ATTN_GROUP, nt),
        in_specs=[
            pl.BlockSpec((None, ATTN_GROUP, S, LANES), lambda b, h, i: (b, h, 0, 0)),
            pl.BlockSpec((None, ATTN_GROUP, S, LANES), lambda b, h, i: (b, h, 0, 0)),
            pl.BlockSpec((None, nt, ATTN_GROUP * MLA_V, tm), lambda b, h, i: (b, 0, h, 0)),
        ],
        out_specs=pl.BlockSpec((None, ATTN_GROUP * MLA_V, tm), lambda b, h, i: (b, h, i)),
        out_shape=jax.ShapeDtypeStruct((B, H * MLA_V, S), BF16),
        scratch_shapes=[pltpu.SMEM((1,), jnp.int32),
                        pltpu.VMEM((2, ATTN_GROUP, tm, tm), F32),
                        pltpu.VMEM((ATTN_GROUP, MLA_V + ATTN_ONES_ROWS, tm), F32)],
        compiler_params=pltpu.CompilerParams(
            dimension_semantics=("arbitrary", "arbitrary", "arbitrary"),
            vmem_limit_bytes=VMEM_LIMIT),
    )(q, kh, vt)

    out = pl.pallas_call(
        functools.partial(_out_kernel, alpha=alpha, ff_chunk=_pick(D_FF, 1024), parts=to // tm),
        grid=(B, S // to),
        in_specs=[
            pl.BlockSpec((None, to, D), lambda b, j: (b, j, 0)),
            pl.BlockSpec((None, to, RWKV_WIDTH), lambda b, j: (b, j, 0)),
            pl.BlockSpec((None, MLA_WIDTH, to), lambda b, j: (b, 0, j)),
            _const_spec(wo_r.shape), _const_spec(wo_m.shape), _const_spec(ln1.shape),
            _const_spec(w1.shape), _const_spec(w2.shape), _const_spec(ln2.shape),
        ],
        out_specs=pl.BlockSpec((None, to, D), lambda b, j: (b, j, 0)),
        out_shape=jax.ShapeDtypeStruct((B, S, D), F32),
        compiler_params=pltpu.CompilerParams(
            dimension_semantics=("arbitrary", "arbitrary"), vmem_limit_bytes=VMEM_LIMIT),
    )(x, y_rwkv, y_mla_t, wo_r, wo_m, ln1, w1, w2, ln2)
    return out


def kernel(x, positions, w_in, shift_mu, decay_w0, decay_w2, iclr_a0, iclr_a2, gate_g2, k_k, k_a, r_k, lnx_g, lnx_b, q_norm_g, w_uq, kv_norm_g, w_ukv, w_out, ln1_g, ln1_b, w_ffn1, w_ffn2, ln2_g, ln2_b):
    depth = w_in.shape[0]
    alpha = (2 * depth) ** 0.25
    params = (w_in, shift_mu, decay_w0, decay_w2, iclr_a0, iclr_a2, gate_g2, k_k, k_a, r_k, lnx_g,
              lnx_b, q_norm_g, w_uq, kv_norm_g, w_ukv, w_out, ln1_g, ln1_b, w_ffn1, w_ffn2,
              ln2_g, ln2_b)
    for l in range(depth):
        x = _layer(x, positions, *(p[l] for p in params), alpha)
    return x
```

```python
import functools
import math

import jax
import jax.numpy as jnp
from jax import lax
from jax.experimental import pallas as pl
from jax.experimental.pallas import tpu as pltpu

F32 = jnp.float32
BF16 = jnp.bfloat16

D_MODEL = 1024
RWKV_HEAD = 64
RWKV_HEADS = 8
RWKV_WIDTH = RWKV_HEADS * RWKV_HEAD
DECAY_LORA = 64
ICLR_LORA = 64
GATE_LORA = 128
MLA_HEADS = 8
MLA_NOPE = 64
MLA_ROPE = 32
MLA_V = 64
MLA_WIDTH = MLA_HEADS * MLA_V
Q_LORA = 384
KV_LORA = 256
ROPE_BASE = 10000.0
D_FF = 4 * D_MODEL
LN_EPS = 1e-5
RMS_EPS = 1e-6
GN_EPS = 64e-5
RWKV_COLS = 3 * RWKV_WIDTH + DECAY_LORA + ICLR_LORA + GATE_LORA
MLA_COLS = Q_LORA + KV_LORA + MLA_ROPE

LANES = 128
VMEM_LIMIT = 56 * 1024 * 1024

C_R, C_K, C_V = 0, RWKV_WIDTH, 2 * RWKV_WIDTH
C_WA = 3 * RWKV_WIDTH
C_G = C_WA + DECAY_LORA + ICLR_LORA
C_Q = RWKV_COLS
C_KV = C_Q + Q_LORA
C_PE = C_KV + KV_LORA
IN_COLS_PAD = C_PE + LANES
PE_LO = MLA_NOPE
PE_HALF = MLA_ROPE // 2

PREP_PARTS = 2
CHUNK = 64
ATTN_GROUP = 4
ATTN_ONES_ROWS = 16
F32_MAX_EXP2 = 126.0
ATTN_BOUND_SLACK = 1.01


def _dot(a, b):
    return jnp.dot(a.astype(BF16), b.astype(BF16), preferred_element_type=F32)


def _split_bf16(x):
    hi = x.astype(BF16)
    lo = (x - hi.astype(F32)).astype(BF16)
    return hi, lo


def _sigmoid(z):
    return 1.0 / (1.0 + jnp.exp(-z))


def _layer_norm(x, g, b):
    mu = jnp.mean(x, axis=-1, keepdims=True)
    d = x - mu
    var = jnp.mean(d * d, axis=-1, keepdims=True)
    return d * lax.rsqrt(var + LN_EPS) * g + b


def _prep_kernel(x_ref, pos_ref, w_ref, mu_ref, lora_ref, w0a0_ref, g2_ref, qg_ref, wq_ref,
                 kvg_ref, wk_ref, wvt_ref, freq_ref,
                 r_ref, k_ref, v_ref, lw_ref, a_ref, g_ref, q_ref, kh_ref, vt_ref, carry_ref,
                 *, scale, parts):
    tm = x_ref.shape[0]
    j = pl.program_id(1)

    @pl.when(j == 0)
    def _():
        carry_ref[...] = jnp.zeros_like(carry_ref)

    rows = tm // parts
    part = [slice(p * rows, (p + 1) * rows) for p in range(parts)]
    hs = [jnp.dot(x_ref[sl, :].astype(BF16), w_ref[...], preferred_element_type=F32)
          for sl in part]

    row = lax.broadcasted_iota(jnp.int32, (rows, RWKV_COLS), 0)
    lane = lax.broadcasted_iota(jnp.int32, (rows, LANES), 1)
    first_half = (lane >= PE_LO) & (lane < PE_LO + PE_HALF)

    def rms(c, gain):
        return c * lax.rsqrt(jnp.mean(c * c, axis=-1, keepdims=True) + RMS_EPS) * gain

    second_half = (lane >= PE_LO + PE_HALF) & (lane < PE_LO + MLA_ROPE)
    packed = rows // 8
    angs = [pos_ref[p * packed:(p + 1) * packed, :].astype(F32) * freq_ref[...]
            for p in range(parts)]

    def own_lanes(t):
        x = jnp.broadcast_to(t[:, None, :], (packed, 8, LANES)).reshape(rows, LANES)
        return pltpu.roll(x, 0, 1, stride=PE_HALF, stride_axis=0)

    prev_row = carry_ref[0:1, :]
    for p, (rs, h) in enumerate(zip(part, hs)):
        hr = h[:, :RWKV_COLS]
        prev = jnp.where(row == 0, prev_row, pltpu.roll(hr, 1, axis=0))
        prev_row = hr[rows - 1:rows, :]
        hr = hr + (prev - hr) * mu_ref[...]

        r_ref[rs, :] = hr[:, C_R:C_R + RWKV_WIDTH]
        k_ref[rs, :] = hr[:, C_K:C_K + RWKV_WIDTH]
        v_ref[rs, :] = hr[:, C_V:C_V + RWKV_WIDTH]

        xwa = hr[:, C_WA:C_WA + LANES]
        xwa = jnp.where(lane < DECAY_LORA, jnp.tanh(xwa), xwa)
        z = _dot(xwa, lora_ref[...]) + w0a0_ref[...]
        sg = _sigmoid(z)
        lw_ref[rs, :] = (-math.exp(-0.5)) * sg[:, :RWKV_WIDTH]
        a_ref[rs, :] = sg[:, RWKV_WIDTH:]
        g_ref[rs, :] = _dot(_sigmoid(hr[:, C_G:C_G + GATE_LORA]), g2_ref[...])

        cqn = rms(h[:, C_Q:C_Q + Q_LORA], qg_ref[...]).astype(BF16)
        ckvn = rms(h[:, C_KV:C_KV + KV_LORA], kvg_ref[...]).astype(BF16)
        q_all = jnp.dot(cqn, wq_ref[...], preferred_element_type=F32)
        k_all = jnp.dot(ckvn, wk_ref[...], preferred_element_type=F32)
        vt = lax.dot_general(wvt_ref[...], ckvn, (((1,), (1,)), ((), ())),
                             preferred_element_type=F32)
        vt_ref[:, rs] = vt.astype(vt_ref.dtype)

        cos_own, sin_own = own_lanes(jnp.cos(angs[p])), own_lanes(jnp.sin(angs[p]))
        cosf = jnp.where(first_half, cos_own,
                         jnp.where(second_half, pltpu.roll(cos_own, PE_HALF, axis=1), 1.0))
        sins = jnp.where(first_half, -sin_own,
                         jnp.where(second_half, pltpu.roll(sin_own, PE_HALF, axis=1), 0.0))

        def rope(zg):
            rot = jnp.where(first_half, pltpu.roll(zg, LANES - PE_HALF, axis=1),
                            pltpu.roll(zg, PE_HALF, axis=1))
            return zg * cosf + rot * sins

        kpe = rope(h[:, C_PE:C_PE + LANES])
        for hd in range(MLA_HEADS):
            sl = slice(hd * LANES, (hd + 1) * LANES)
            q_ref[hd, rs, :] = (rope(q_all[:, sl]) * scale).astype(q_ref.dtype)
            kh_ref[hd, rs, :] = (k_all[:, sl] + kpe).astype(kh_ref.dtype)
    carry_ref[0:1, :] = prev_row


_NN = (((2,), (1,)), ((0,), (0,)))
_NT = (((2,), (2,)), ((0,), (0,)))
_TN = (((1,), (1,)), ((0,), (0,)))


def _rwkv_kernel(r_ref, k_ref, v_ref, lw_ref, a_ref, gprev_ref, par_ref, parprev_ref, o_ref,
                 h_ref, rq_ref, ya_ref, g2_ref, ha_ref, bonus_ref, *, chunk, nsteps):
    tb = r_ref.shape[0]
    C = chunk
    nc = tb // C
    C2 = 2 * C
    s = pl.program_id(0)

    @pl.when(s == 0)
    def _():
        for ref in (h_ref, rq_ref, ya_ref, g2_ref, ha_ref, bonus_ref):
            ref[...] = jnp.zeros_like(ref)

    first_of_sequence = (jnp.maximum(s - 1, 0) % nsteps) == 0
    scan = {"h": h_ref[...] * jnp.where(first_of_sequence, 0.0, 1.0), "ys": []}

    def scan_step():
        c = len(scan["ys"])
        if c < nc:
            hb = scan["h"].astype(BF16)
            y2 = jnp.dot(rq_ref[c], hb, preferred_element_type=F32) + ya_ref[c]
            scan["ys"].append(y2[:C] + y2[C:])
            scan["h"] = jnp.dot(g2_ref[c], hb, preferred_element_type=F32) + ha_ref[c]

    lane = lax.broadcasted_iota(jnp.int32, (1, LANES), 1)
    m0 = (lane < RWKV_HEAD).astype(F32)
    m1 = 1.0 - m0
    rr = lax.broadcasted_iota(jnp.int32, (LANES, LANES), 0)
    cc = lax.broadcasted_iota(jnp.int32, (LANES, LANES), 1)
    ones2 = ((rr < RWKV_HEAD) == (cc < RWKV_HEAD)).astype(BF16)
    eye = (rr == cc).astype(F32)

    k_k, k_a, r_k = par_ref[0:1, :], par_ref[1:2, :], par_ref[2:3, :]
    ln_g, ln_b = parprev_ref[3:4, :], parprev_ref[4:5, :]

    scan_step()
    r = r_ref[...]
    k = k_ref[...]
    v = v_ref[...]
    lw = lw_ref[...]
    a = a_ref[...]

    kk0 = k * k_k
    n2 = _dot(kk0 * kk0, ones2)
    kk = kk0 / jnp.maximum(jnp.sqrt(n2), 1e-12)
    kmod = k * (1.0 + (a - 1.0) * k_a)
    bonus = _dot(r * kmod * r_k, ones2) * v
    bb = kk * a

    to3 = lambda t: t.reshape(nc, C, LANES)
    r3, k3, v3, lw3, kk3, bb3 = to3(r), to3(kmod), to3(v), to3(lw), to3(kk), to3(bb)

    ti = lax.broadcasted_iota(jnp.int32, (nc, C, C), 1)
    si = lax.broadcasted_iota(jnp.int32, (nc, C, C), 2)
    tril = (si <= ti).astype(BF16)
    lhi, llo = _split_bf16(lw3)
    cum = (lax.dot_general(tril, lhi, _NN, preferred_element_type=F32)
           + lax.dot_general(tril, llo, _NN, preferred_element_type=F32))
    scan_step()
    cend = cum[:, C - 1:C, :]
    p_inc = jnp.exp(cum)
    p_exc = jnp.exp(cum - lw3)
    p_inv = jnp.exp(-cum)
    p_end = jnp.exp(cend - cum)
    p_all = jnp.exp(cend)

    m0b, m1b = m0.astype(BF16), m1.astype(BF16)

    def expand(t):
        tb = t.astype(BF16)
        return jnp.concatenate([tb * m0b, tb * m1b], axis=1)

    rt = r3 * p_inc
    at2 = expand(-kk3 * p_exc)
    rt2 = expand(rt)
    bt2 = expand(bb3 * p_inv)
    kt2 = expand(k3 * p_inv)
    v2 = expand(v3)
    bbar2 = expand(bb3 * p_end)
    kbar2 = expand(k3 * p_end)

    bdot = functools.partial(lax.dot_general, preferred_element_type=F32)
    lhs = jnp.concatenate([at2, rt2], axis=1)
    rhs = jnp.concatenate([bt2, kt2], axis=1)
    m_all = bdot(lhs, rhs, _NT)
    ri = lax.broadcasted_iota(jnp.int32, (4 * C, 4 * C), 0)
    ci = lax.broadcasted_iota(jnp.int32, (4 * C, 4 * C), 1)
    bottom = (ri >= C2).astype(jnp.int32)
    right = (ci >= C2).astype(jnp.int32)
    keep = (ri - C2 * bottom) - (ci - C2 * right) + bottom > 0
    m_all = jnp.where(keep[None], m_all, 0.0)
    a_ab = m_all[:, :C2, :C2]
    a_ak = m_all[:, :C2, C2:].astype(BF16)
    a_r = m_all[:, C2:, :].astype(BF16)
    scan_step()

    rounds = max(1, (C - 1).bit_length()) - 1
    tmat = eye[None] + a_ab
    pw = a_ab.astype(BF16)
    pw = bdot(pw, pw, _NN).astype(BF16)
    scan_step()
    for rnd in range(rounds):
        if rnd + 1 < rounds:
            both = bdot(pw, jnp.concatenate([tmat.astype(BF16), pw], axis=2), _NN)
            tmat = tmat + both[:, :, :C2]
            pw = both[:, :, C2:].astype(BF16)
        else:
            tmat = tmat + bdot(pw, tmat.astype(BF16), _NN)
        scan_step()

    akv = bdot(a_ak, v2, _NN).astype(BF16)
    scan_step()
    wu = bdot(tmat.astype(BF16), jnp.concatenate([at2, akv], axis=2), _NN)
    wu = wu.astype(BF16)
    while len(scan["ys"]) < nc:
        scan_step()
    h_ref[...] = scan["h"]

    y = jnp.concatenate(scan["ys"], axis=0)
    inv_n = 1.0 / RWKV_HEAD
    mu = _dot(y, ones2) * inv_n
    d = y - mu
    var = _dot(d * d, ones2) * inv_n
    yn = d * lax.rsqrt(var + GN_EPS) * ln_g + ln_b
    o_ref[...] = ((yn + bonus_ref[...]) * gprev_ref[...]).astype(o_ref.dtype)

    zmat = jnp.concatenate(
        [wu, jnp.concatenate([jnp.zeros_like(v2), v2], axis=2)], axis=1)
    ry = bdot(a_r, zmat, _NN)
    bk = jnp.concatenate([bbar2, kbar2], axis=1)
    gh = bdot(bk, zmat, _TN)

    rq_ref[...] = (jnp.concatenate([rt * m0, rt * m1], axis=1) + ry[:, :, :LANES]).astype(BF16)
    ya_ref[...] = ry[:, :, LANES:]
    g2_ref[...] = (gh[:, :, :LANES] + eye[None] * p_all).astype(BF16)
    ha_ref[...] = gh[:, :, LANES:]
    bonus_ref[...] = bonus


def _attn_kernel(q_ref, k_ref, vt_ref, o_ref, fast_ref, st_ref, acc_ref):
    nheads, seq, _ = q_ref.shape
    t = vt_ref.shape[2]
    i = pl.program_id(2)
    dn = (((1,), (1,)), ((), ()))
    sub = 8

    @pl.when(i == 0)
    def _():
        worst = []
        for hd in range(nheads):
            ka = jnp.abs(k_ref[hd].astype(F32)).reshape(seq // sub, sub, LANES)
            kmax = jnp.max(ka, axis=0).astype(BF16)
            bound = lax.dot_general(kmax, jnp.abs(q_ref[hd]), dn, preferred_element_type=F32)
            worst.append(jnp.max(bound) * ATTN_BOUND_SLACK)
        vmax = jnp.max(jnp.abs(vt_ref[...].astype(F32)))
        vlog = jnp.max(jnp.log2(jnp.maximum(jnp.full((sub, LANES), vmax), 1.0)))
        budget = F32_MAX_EXP2 - math.log2(seq)
        fast_ref[0] = (functools.reduce(jnp.maximum, worst) + vlog <= budget).astype(jnp.int32)

    fast_ok = fast_ref[0] == 1

    def scores(hd, jb):
        kb = k_ref[hd, pl.ds(pl.multiple_of(jb * t, t), t), :]
        qb = q_ref[hd, pl.ds(pl.multiple_of(i * t, t), t), :]
        return lax.dot_general(kb, qb, dn, preferred_element_type=F32)

    def causal(x):
        krow = lax.broadcasted_iota(jnp.int32, x.shape, 0)
        qcol = lax.broadcasted_iota(jnp.int32, x.shape, 1)
        return jnp.where(krow <= qcol, x, -jnp.inf)

    @pl.when(fast_ok)
    def _():
        ones_v = jnp.ones((ATTN_ONES_ROWS, t), BF16)
        acc_ref[...] = jnp.zeros_like(acc_ref)

        def produce(jb, slot):
            for hd in range(nheads):
                st_ref[slot, hd] = scores(hd, jb)

        def consume(jb, slot, masked):
            for hd in range(nheads):
                x = st_ref[slot, hd]
                p = jnp.exp2(causal(x) if masked else x).astype(BF16)
                vaug = jnp.concatenate(
                    [vt_ref[jb, hd * MLA_V:(hd + 1) * MLA_V, :], ones_v], axis=0)
                acc_ref[hd] += jnp.dot(vaug, p, preferred_element_type=F32)

        produce(0, 0)

        def pair(jj, c):
            j0 = 2 * jj
            produce(j0 + 1, 1)
            consume(j0, 0, False)
            produce(j0 + 2, 0)
            consume(j0 + 1, 1, False)
            return c

        lax.fori_loop(0, i // 2, pair, 0)

        @pl.when(i % 2 == 1)
        def _():
            produce(i, 1)
            consume(i - 1, 0, False)
            consume(i, 1, True)

        @pl.when(i % 2 == 0)
        def _():
            consume(i, 0, True)

        for hd in range(nheads):
            acc = acc_ref[hd]
            o_ref[hd * MLA_V:(hd + 1) * MLA_V, :] = (
                acc[:MLA_V] / acc[MLA_V:MLA_V + 1]).astype(o_ref.dtype)

    @pl.when(jnp.logical_not(fast_ok))
    def _():
        def block(jb, carry, masked):
            out = []
            for hd in range(nheads):
                m, l, acc = carry[hd]
                st = scores(hd, jb)
                if masked:
                    st = causal(st)
                mn = jnp.maximum(m, jnp.max(st, axis=0, keepdims=True))
                alpha = jnp.exp2(m - mn)
                p = jnp.exp2(st - mn)
                l = l * alpha + jnp.sum(p, axis=0, keepdims=True)
                vb = vt_ref[jb, hd * MLA_V:(hd + 1) * MLA_V, :]
                acc = acc * alpha + jnp.dot(vb, p.astype(BF16), preferred_element_type=F32)
                out.append((mn, l, acc))
            return tuple(out)

        init = tuple((jnp.full((1, t), -jnp.inf, F32), jnp.zeros((1, t), F32),
                      jnp.zeros((MLA_V, t), F32)) for _ in range(nheads))
        carry = lax.fori_loop(0, i, lambda jb, c: block(jb, c, False), init)
        carry = block(i, carry, True)
        for hd in range(nheads):
            _, l, acc = carry[hd]
            o_ref[hd * MLA_V:(hd + 1) * MLA_V, :] = (acc / l).astype(o_ref.dtype)


def _out_kernel(x_ref, yr_ref, ymt_ref, wor_ref, wom_ref, ln1_ref, w1_ref, w2_ref, ln2_ref, o_ref,
                *, alpha, ff_chunk, parts):
    rows = x_ref.shape[0] // parts
    part = [slice(p * rows, (p + 1) * rows) for p in range(parts)]
    mixes = []
    for sl in part:
        mix = jnp.dot(yr_ref[sl, :], wor_ref[...], preferred_element_type=F32)
        mixes.append(mix + lax.dot_general(ymt_ref[:, sl], wom_ref[...], (((0,), (0,)), ((), ())),
                                           preferred_element_type=F32))
    x1s = [_layer_norm(alpha * x_ref[sl, :] + mix, ln1_ref[0:1, :], ln1_ref[1:2, :])
           for sl, mix in zip(part, mixes)]
    for sl, x1 in zip(part, x1s):
        x1b = x1.astype(BF16)
        f = jnp.zeros_like(x1)
        for c in range(0, w1_ref.shape[1], ff_chunk):
            hid = jnp.dot(x1b, w1_ref[:, c:c + ff_chunk], preferred_element_type=F32)
            hid = jnp.square(jnp.maximum(hid, 0.0)).astype(BF16)
            f = f + jnp.dot(hid, w2_ref[c:c + ff_chunk, :], preferred_element_type=F32)
        o_ref[sl, :] = _layer_norm(alpha * x1 + f, ln2_ref[0:1, :], ln2_ref[1:2, :])


def _const_spec(shape):
    nd = len(shape)
    return pl.BlockSpec(shape, lambda *_: (0,) * nd, pipeline_mode=pl.Buffered(1))


def _pick(n, pref):
    t = min(pref, n)
    while n % t:
        t //= 2
    return t


def _layer(x, positions, w_in, shift_mu, decay_w0, decay_w2, iclr_a0, iclr_a2, gate_g2, k_k, k_a,
           r_k, lnx_g, lnx_b, q_norm_g, w_uq, kv_norm_g, w_ukv, w_out, ln1_g, ln1_b, w_ffn1,
           w_ffn2, ln2_g, ln2_b, alpha):
    B, S, D = x.shape
    T = B * S
    H = MLA_HEADS
    tm = _pick(S, 512)
    tb = _pick(S, 512)
    to = _pick(S, 2 * tm)
    chunk = _pick(tb, CHUNK)
    nt = S // tm

    w_r = w_in[:, :RWKV_COLS]
    w_m = w_in[:, RWKV_COLS:]
    w_pe = jnp.zeros((D, LANES), F32).at[:, PE_LO:PE_LO + MLA_ROPE].set(w_m[:, Q_LORA + KV_LORA:])
    w_all = jnp.concatenate([w_r, w_m[:, :Q_LORA + KV_LORA], w_pe], axis=1).astype(BF16)
    mu = shift_mu.reshape(1, RWKV_COLS)
    lora = jnp.zeros((LANES, 2 * RWKV_WIDTH), F32)
    lora = lora.at[:DECAY_LORA, :RWKV_WIDTH].set(decay_w2).at[DECAY_LORA:, RWKV_WIDTH:].set(iclr_a2)
    lora = lora.astype(BF16)
    w0a0 = jnp.concatenate([decay_w0, iclr_a0]).reshape(1, 2 * RWKV_WIDTH)
    g2 = gate_g2.astype(BF16)
    dq = MLA_NOPE + MLA_ROPE
    wq = jnp.zeros((Q_LORA, H, LANES), F32).at[:, :, :dq].set(w_uq.reshape(Q_LORA, H, dq))
    wq = wq.reshape(Q_LORA, H * LANES).astype(BF16)
    wkv = w_ukv.reshape(KV_LORA, H, MLA_NOPE + MLA_V)
    wk = jnp.zeros((KV_LORA, H, LANES), F32).at[:, :, :MLA_NOPE].set(wkv[:, :, :MLA_NOPE])
    wk = wk.reshape(KV_LORA, H * LANES).astype(BF16)
    wvt = wkv[:, :, MLA_NOPE:].reshape(KV_LORA, H * MLA_V).T.astype(BF16)
    inv_freq = ROPE_BASE ** (-jnp.arange(0, MLA_ROPE, 2, dtype=F32) / MLA_ROPE)
    freq = jnp.tile(inv_freq, LANES // PE_HALF).reshape(1, LANES)
    par = jnp.zeros((8, RWKV_WIDTH), F32)
    par = par.at[0].set(k_k).at[1].set(k_a).at[2].set(r_k.reshape(-1)).at[3].set(lnx_g).at[4].set(lnx_b)
    ln1 = jnp.stack([ln1_g, ln1_b])
    ln2 = jnp.stack([ln2_g, ln2_b])
    wo_r = w_out[:RWKV_WIDTH].astype(BF16)
    wo_m = w_out[RWKV_WIDTH:].astype(BF16)
    w1 = w_ffn1.astype(BF16)
    w2 = w_ffn2.astype(BF16)

    order = [(PE_LO // PE_HALF - b) % 8 for b in range(8)]
    pos_packed = jnp.repeat(positions.reshape(B, S // 8, 8)[:, :, order], PE_HALF, axis=-1)

    tok = lambda: pl.BlockSpec((None, tm, RWKV_WIDTH), lambda b, j: (b, j, 0))
    tok_shape = jax.ShapeDtypeStruct((B, S, RWKV_WIDTH), F32)
    head_spec = pl.BlockSpec((None, H, tm, LANES), lambda b, j: (b, 0, j, 0))
    r, k, v, lw, a, g, q, kh, vt = pl.pallas_call(
        functools.partial(_prep_kernel, scale=float(dq) ** -0.5 * math.log2(math.e),
                          parts=PREP_PARTS),
        grid=(B, nt),
        in_specs=[
            pl.BlockSpec((None, tm, D), lambda b, j: (b, j, 0)),
            pl.BlockSpec((None, tm // 8, LANES), lambda b, j: (b, j, 0)),
            _const_spec(w_all.shape), _const_spec(mu.shape), _const_spec(lora.shape),
            _const_spec(w0a0.shape), _const_spec(g2.shape),
            _const_spec((1, Q_LORA)), _const_spec(wq.shape),
            _const_spec((1, KV_LORA)), _const_spec(wk.shape), _const_spec(wvt.shape),
            _const_spec(freq.shape),
        ],
        out_specs=[tok(), tok(), tok(), tok(), tok(), tok(), head_spec, head_spec,
                   pl.BlockSpec((None, None, H * MLA_V, tm), lambda b, j: (b, j, 0, 0))],
        out_shape=[tok_shape] * 6 + [
            jax.ShapeDtypeStruct((B, H, S, LANES), BF16),
            jax.ShapeDtypeStruct((B, H, S, LANES), BF16),
            jax.ShapeDtypeStruct((B, nt, H * MLA_V, tm), BF16)],
        scratch_shapes=[pltpu.VMEM((8, RWKV_COLS), F32)],
        compiler_params=pltpu.CompilerParams(
            dimension_semantics=("arbitrary", "arbitrary"), vmem_limit_bytes=VMEM_LIMIT),
    )(x, pos_packed, w_all, mu, lora, w0a0, g2, q_norm_g.reshape(1, Q_LORA), wq,
      kv_norm_g.reshape(1, KV_LORA), wk, wvt, freq)

    npair = RWKV_WIDTH // LANES
    nsteps = S // tb
    nblocks = B * npair * nsteps
    nc = tb // chunk

    def block_of(s):
        return s // (npair * nsteps), s % nsteps, (s // nsteps) % npair

    cur = lambda: pl.BlockSpec((None, tb, LANES), lambda s: block_of(jnp.minimum(s, nblocks - 1)))
    prev = lambda: pl.BlockSpec((None, tb, LANES), lambda s: block_of(jnp.maximum(s - 1, 0)))
    y_rwkv = pl.pallas_call(
        functools.partial(_rwkv_kernel, chunk=chunk, nsteps=nsteps),
        grid=(nblocks + 1,),
        in_specs=[cur(), cur(), cur(), cur(), cur(), prev(),
                  pl.BlockSpec((8, LANES), lambda s: (0, block_of(jnp.minimum(s, nblocks - 1))[2])),
                  pl.BlockSpec((8, LANES), lambda s: (0, block_of(jnp.maximum(s - 1, 0))[2]))],
        out_specs=prev(),
        out_shape=jax.ShapeDtypeStruct((B, S, RWKV_WIDTH), BF16),
        scratch_shapes=[pltpu.VMEM((LANES, LANES), F32),
                        pltpu.VMEM((nc, 2 * chunk, LANES), BF16),
                        pltpu.VMEM((nc, 2 * chunk, LANES), F32),
                        pltpu.VMEM((nc, LANES, LANES), BF16),
                        pltpu.VMEM((nc, LANES, LANES), F32),
                        pltpu.VMEM((tb, LANES), F32)],
        compiler_params=pltpu.CompilerParams(
            dimension_semantics=("arbitrary",), vmem_limit_bytes=VMEM_LIMIT),
    )(r, k, v, lw, a, g, par, par)

    y_mla_t = pl.pallas_call(
        _attn_kernel,
        grid=(B, H // ATTN_GROUP, nt),
        in_specs=[
            pl.BlockSpec((None, ATTN_GROUP, S, LANES), lambda b, h, i: (b, h, 0, 0)),
            pl.BlockSpec((None, ATTN_GROUP, S, LANES), lambda b, h, i: (b, h, 0, 0)),
            pl.BlockSpec((None, nt, ATTN_GROUP * MLA_V, tm), lambda b, h, i: (b, 0, h, 0)),
        ],
        out_specs=pl.BlockSpec((None, ATTN_GROUP * MLA_V, tm), lambda b, h, i: (b, h, i)),
        out_shape=jax.ShapeDtypeStruct((B, H * MLA_V, S), BF16),
        scratch_shapes=[pltpu.SMEM((1,), jnp.int32),
                        pltpu.VMEM((2, ATTN_GROUP, tm, tm), F32),
                        pltpu.VMEM((ATTN_GROUP, MLA_V + ATTN_ONES_ROWS, tm), F32)],
        compiler_params=pltpu.CompilerParams(
            dimension_semantics=("arbitrary", "arbitrary", "arbitrary"),
            vmem_limit_bytes=VMEM_LIMIT),
    )(q, kh, vt)

    out = pl.pallas_call(
        functools.partial(_out_kernel, alpha=alpha, ff_chunk=_pick(D_FF, 1024), parts=to // tm),
        grid=(B, S // to),
        in_specs=[
            pl.BlockSpec((None, to, D), lambda b, j: (b, j, 0)),
            pl.BlockSpec((None, to, RWKV_WIDTH), lambda b, j: (b, j, 0)),
            pl.BlockSpec((None, MLA_WIDTH, to), lambda b, j: (b, 0, j)),
            _const_spec(wo_r.shape), _const_spec(wo_m.shape), _const_spec(ln1.shape),
            _const_spec(w1.shape), _const_spec(w2.shape), _const_spec(ln2.shape),
        ],
        out_specs=pl.BlockSpec((None, to, D), lambda b, j: (b, j, 0)),
        out_shape=jax.ShapeDtypeStruct((B, S, D), F32),
        compiler_params=pltpu.CompilerParams(
            dimension_semantics=("arbitrary", "arbitrary"), vmem_limit_bytes=VMEM_LIMIT),
    )(x, y_rwkv, y_mla_t, wo_r, wo_m, ln1, w1, w2, ln2)
    return out


def kernel(x, positions, w_in, shift_mu, decay_w0, decay_w2, iclr_a0, iclr_a2, gate_g2, k_k, k_a, r_k, lnx_g, lnx_b, q_norm_g, w_uq, kv_norm_g, w_ukv, w_out, ln1_g, ln1_b, w_ffn1, w_ffn2, ln2_g, ln2_b):
    depth = w_in.shape[0]
    alpha = (2 * depth) ** 0.25
    params = (w_in, shift_mu, decay_w0, decay_w2, iclr_a0, iclr_a2, gate_g2, k_k, k_a, r_k, lnx_g,
              lnx_b, q_norm_g, w_uq, kv_norm_g, w_ukv, w_out, ln1_g, ln1_b, w_ffn1, w_ffn2,
              ln2_g, ln2_b)
    for l in range(depth):
        x = _layer(x, positions, *(p[l] for p in params), alpha)
    return x
```

```python
import functools
import math

import jax
import jax.numpy as jnp
from jax import lax
from jax.experimental import pallas as pl
from jax.experimental.pallas import tpu as pltpu

F32 = jnp.float32
BF16 = jnp.bfloat16

D_MODEL = 1024
RWKV_HEAD = 64
RWKV_HEADS = 8
RWKV_WIDTH = RWKV_HEADS * RWKV_HEAD
DECAY_LORA = 64
ICLR_LORA = 64
GATE_LORA = 128
MLA_HEADS = 8
MLA_NOPE = 64
MLA_ROPE = 32
MLA_V = 64
MLA_WIDTH = MLA_HEADS * MLA_V
Q_LORA = 384
KV_LORA = 256
ROPE_BASE = 10000.0
D_FF = 4 * D_MODEL
LN_EPS = 1e-5
RMS_EPS = 1e-6
GN_EPS = 64e-5
RWKV_COLS = 3 * RWKV_WIDTH + DECAY_LORA + ICLR_LORA + GATE_LORA
MLA_COLS = Q_LORA + KV_LORA + MLA_ROPE

LANES = 128
VMEM_LIMIT = 56 * 1024 * 1024

C_R, C_K, C_V = 0, RWKV_WIDTH, 2 * RWKV_WIDTH
C_WA = 3 * RWKV_WIDTH
C_G = C_WA + DECAY_LORA + ICLR_LORA
C_Q = RWKV_COLS
C_KV = C_Q + Q_LORA
C_PE = C_KV + KV_LORA
IN_COLS_PAD = C_PE + LANES
PE_LO = MLA_NOPE
PE_HALF = MLA_ROPE // 2

PREP_PARTS = 2
CHUNK = 64
ATTN_GROUP = 4
ATTN_ONES_ROWS = 16
F32_MAX_EXP2 = 126.0
ATTN_BOUND_SLACK = 1.01


def _dot(a, b):
    return jnp.dot(a.astype(BF16), b.astype(BF16), preferred_element_type=F32)


def _split_bf16(x):
    hi = x.astype(BF16)
    lo = (x - hi.astype(F32)).astype(BF16)
    return hi, lo


def _sigmoid(z):
    return 1.0 / (1.0 + jnp.exp(-z))


def _layer_norm(x, g, b):
    mu = jnp.mean(x, axis=-1, keepdims=True)
    d = x - mu
    var = jnp.mean(d * d, axis=-1, keepdims=True)
    return d * lax.rsqrt(var + LN_EPS) * g + b


def _prep_kernel(x_ref, pos_ref, w_ref, mu_ref, lora_ref, w0a0_ref, g2_ref, qg_ref, wq_ref,
                 kvg_ref, wk_ref, wvt_ref, freq_ref,
                 r_ref, k_ref, v_ref, lw_ref, a_ref, g_ref, q_ref, kh_ref, vt_ref, carry_ref,
                 *, scale, parts):
    tm = x_ref.shape[0]
    j = pl.program_id(1)

    @pl.when(j == 0)
    def _():
        carry_ref[...] = jnp.zeros_like(carry_ref)

    rows = tm // parts
    part = [slice(p * rows, (p + 1) * rows) for p in range(parts)]
    hs = [jnp.dot(x_ref[sl, :].astype(BF16), w_ref[...], preferred_element_type=F32)
          for sl in part]

    row = lax.broadcasted_iota(jnp.int32, (rows, RWKV_COLS), 0)
    lane = lax.broadcasted_iota(jnp.int32, (rows, LANES), 1)
    first_half = (lane >= PE_LO) & (lane < PE_LO + PE_HALF)

    def rms(c, gain):
        return c * lax.rsqrt(jnp.mean(c * c, axis=-1, keepdims=True) + RMS_EPS) * gain

    second_half = (lane >= PE_LO + PE_HALF) & (lane < PE_LO + MLA_ROPE)
    packed = rows // 8
    angs = [pos_ref[p * packed:(p + 1) * packed, :].astype(F32) * freq_ref[...]
            for p in range(parts)]

    def own_lanes(t):
        x = jnp.broadcast_to(t[:, None, :], (packed, 8, LANES)).reshape(rows, LANES)
        return pltpu.roll(x, 0, 1, stride=PE_HALF, stride_axis=0)

    prev_row = carry_ref[0:1, :]
    for p, (rs, h) in enumerate(zip(part, hs)):
        hr = h[:, :RWKV_COLS]
        prev = jnp.where(row == 0, prev_row, pltpu.roll(hr, 1, axis=0))
        prev_row = hr[rows - 1:rows, :]
        hr = hr + (prev - hr) * mu_ref[...]

        r_ref[rs, :] = hr[:, C_R:C_R + RWKV_WIDTH]
        k_ref[rs, :] = hr[:, C_K:C_K + RWKV_WIDTH]
        v_ref[rs, :] = hr[:, C_V:C_V + RWKV_WIDTH]

        xwa = hr[:, C_WA:C_WA + LANES]
        xwa = jnp.where(lane < DECAY_LORA, jnp.tanh(xwa), xwa)
        z = _dot(xwa, lora_ref[...]) + w0a0_ref[...]
        sg = _sigmoid(z)
        lw_ref[rs, :] = (-math.exp(-0.5)) * sg[:, :RWKV_WIDTH]
        a_ref[rs, :] = sg[:, RWKV_WIDTH:]
        g_ref[rs, :] = _dot(_sigmoid(hr[:, C_G:C_G + GATE_LORA]), g2_ref[...])

        cqn = rms(h[:, C_Q:C_Q + Q_LORA], qg_ref[...]).astype(BF16)
        ckvn = rms(h[:, C_KV:C_KV + KV_LORA], kvg_ref[...]).astype(BF16)
        q_all = jnp.dot(cqn, wq_ref[...], preferred_element_type=F32)
        k_all = jnp.dot(ckvn, wk_ref[...], preferred_element_type=F32)
        vt = lax.dot_general(wvt_ref[...], ckvn, (((1,), (1,)), ((), ())),
                             preferred_element_type=F32)
        vt_ref[:, rs] = vt.astype(vt_ref.dtype)

        cos_own, sin_own = own_lanes(jnp.cos(angs[p])), own_lanes(jnp.sin(angs[p]))
        cosf = jnp.where(first_half, cos_own,
                         jnp.where(second_half, pltpu.roll(cos_own, PE_HALF, axis=1), 1.0))
        sins = jnp.where(first_half, -sin_own,
                         jnp.where(second_half, pltpu.roll(sin_own, PE_HALF, axis=1), 0.0))

        def rope(zg):
            rot = jnp.where(first_half, pltpu.roll(zg, LANES - PE_HALF, axis=1),
                            pltpu.roll(zg, PE_HALF, axis=1))
            return zg * cosf + rot * sins

        kpe = rope(h[:, C_PE:C_PE + LANES])
        for hd in range(MLA_HEADS):
            sl = slice(hd * LANES, (hd + 1) * LANES)
            q_ref[hd, rs, :] = (rope(q_all[:, sl]) * scale).astype(q_ref.dtype)
            kh_ref[hd, rs, :] = (k_all[:, sl] + kpe).astype(kh_ref.dtype)
    carry_ref[0:1, :] = prev_row


_NN = (((2,), (1,)), ((0,), (0,)))
_NT = (((2,), (2,)), ((0,), (0,)))
_TN = (((1,), (1,)), ((0,), (0,)))


def _rwkv_kernel(r_ref, k_ref, v_ref, lw_ref, a_ref, gprev_ref, par_ref, parprev_ref, o_ref,
                 h_ref, rq_ref, ya_ref, g2_ref, ha_ref, bonus_ref, *, chunk, nsteps, parts):
    tb = r_ref.shape[0]
    C = chunk
    nc = tb // C
    C2 = 2 * C
    s = pl.program_id(0)

    @pl.when(s == 0)
    def _():
        for ref in (h_ref, rq_ref, ya_ref, g2_ref, ha_ref, bonus_ref):
            ref[...] = jnp.zeros_like(ref)

    first_of_sequence = (jnp.maximum(s - 1, 0) % nsteps) == 0
    scan = {"h": h_ref[...] * jnp.where(first_of_sequence, 0.0, 1.0), "ys": []}

    def scan_step():
        c = len(scan["ys"])
        if c < nc:
            hb = scan["h"].astype(BF16)
            y2 = jnp.dot(rq_ref[c], hb, preferred_element_type=F32) + ya_ref[c]
            scan["ys"].append(y2[:C] + y2[C:])
            scan["h"] = jnp.dot(g2_ref[c], hb, preferred_element_type=F32) + ha_ref[c]

    lane = lax.broadcasted_iota(jnp.int32, (1, LANES), 1)
    m0 = (lane < RWKV_HEAD).astype(F32)
    m1 = 1.0 - m0
    m0b, m1b = m0.astype(BF16), m1.astype(BF16)
    rr = lax.broadcasted_iota(jnp.int32, (LANES, LANES), 0)
    cc = lax.broadcasted_iota(jnp.int32, (LANES, LANES), 1)
    ones2 = ((rr < RWKV_HEAD) == (cc < RWKV_HEAD)).astype(BF16)
    eye = (rr == cc).astype(F32)
    bdot = functools.partial(lax.dot_general, preferred_element_type=F32)

    k_k, k_a, r_k = par_ref[0:1, :], par_ref[1:2, :], par_ref[2:3, :]
    ln_g, ln_b = parprev_ref[3:4, :], parprev_ref[4:5, :]

    nh = nc // parts
    th = tb // parts

    def post():
        h_ref[...] = scan["h"]
        y = jnp.concatenate(scan["ys"], axis=0)
        inv_n = 1.0 / RWKV_HEAD
        mu = _dot(y, ones2) * inv_n
        d = y - mu
        var = _dot(d * d, ones2) * inv_n
        yn = d * lax.rsqrt(var + GN_EPS) * ln_g + ln_b
        o_ref[...] = ((yn + bonus_ref[...]) * gprev_ref[...]).astype(o_ref.dtype)
        scan["posted"] = True

    def scale(p):
        rs = slice(p * th, (p + 1) * th)
        r, k, v, lw, a = r_ref[rs, :], k_ref[rs, :], v_ref[rs, :], lw_ref[rs, :], a_ref[rs, :]
        kk0 = k * k_k
        kk = kk0 / jnp.maximum(jnp.sqrt(_dot(kk0 * kk0, ones2)), 1e-12)
        kmod = k * (1.0 + (a - 1.0) * k_a)
        bonus = _dot(r * kmod * r_k, ones2) * v
        bb = kk * a

        to3 = lambda t: t.reshape(nh, C, LANES)
        r3, k3, v3, lw3, kk3, bb3 = to3(r), to3(kmod), to3(v), to3(lw), to3(kk), to3(bb)

        ti = lax.broadcasted_iota(jnp.int32, (nh, C, C), 1)
        si = lax.broadcasted_iota(jnp.int32, (nh, C, C), 2)
        tril = (si <= ti).astype(BF16)
        lhi, llo = _split_bf16(lw3)
        cum = bdot(tril, lhi, _NN) + bdot(tril, llo, _NN)
        cend = cum[:, C - 1:C, :]
        p_inv = jnp.exp(-cum)
        p_end = jnp.exp(cend - cum)

        def expand(t):
            tbf = t.astype(BF16)
            return jnp.concatenate([tbf * m0b, tbf * m1b], axis=1)

        rt = r3 * jnp.exp(cum)
        at2 = expand(-kk3 * jnp.exp(cum - lw3))
        return dict(
            rt=rt, at2=at2, v2=expand(v3), p_all=jnp.exp(cend), bonus=bonus,
            lhs=jnp.concatenate([at2, expand(rt)], axis=1),
            rhs=jnp.concatenate([expand(bb3 * p_inv), expand(k3 * p_inv)], axis=1),
            bk=jnp.concatenate([expand(bb3 * p_end), expand(k3 * p_end)], axis=1))

    def operators(d, between):
        m_all = bdot(d["lhs"], d["rhs"], _NT)
        between()
        ri = lax.broadcasted_iota(jnp.int32, (4 * C, 4 * C), 0)
        ci = lax.broadcasted_iota(jnp.int32, (4 * C, 4 * C), 1)
        bottom = (ri >= C2).astype(jnp.int32)
        right = (ci >= C2).astype(jnp.int32)
        keep = (ri - C2 * bottom) - (ci - C2 * right) + bottom > 0
        m_all = jnp.where(keep[None], m_all, 0.0)
        a_ab = m_all[:, :C2, :C2]
        a_ak = m_all[:, :C2, C2:].astype(BF16)
        a_r = m_all[:, C2:, :].astype(BF16)

        rounds = max(1, (C - 1).bit_length()) - 1
        tmat = eye[None] + a_ab
        pw = a_ab.astype(BF16)
        pw = bdot(pw, pw, _NN).astype(BF16)
        between()
        for rnd in range(rounds):
            if rnd + 1 < rounds:
                both = bdot(pw, jnp.concatenate([tmat.astype(BF16), pw], axis=2), _NN)
                tmat = tmat + both[:, :, :C2]
                pw = both[:, :, C2:].astype(BF16)
            else:
                tmat = tmat + bdot(pw, tmat.astype(BF16), _NN)
            between()

        v2 = d["v2"]
        akv = bdot(a_ak, v2, _NN).astype(BF16)
        between()
        wu = bdot(tmat.astype(BF16), jnp.concatenate([d["at2"], akv], axis=2), _NN)
        wu = wu.astype(BF16)
        between()
        zmat = jnp.concatenate(
            [wu, jnp.concatenate([jnp.zeros_like(v2), v2], axis=2)], axis=1)
        ry = bdot(a_r, zmat, _NN)
        gh = bdot(d["bk"], zmat, _TN)
        rt = d["rt"]
        return dict(
            rq=(jnp.concatenate([rt * m0, rt * m1], axis=1) + ry[:, :, :LANES]).astype(BF16),
            ya=ry[:, :, LANES:],
            g2=(gh[:, :, :LANES] + eye[None] * d["p_all"]).astype(BF16),
            ha=gh[:, :, LANES:])

    scaled = {0: scale(0)}
    ops = []
    for p in range(parts):
        fired = []

        def between(p=p, fired=fired):
            scan_step()
            if not fired and p + 1 < parts:
                scaled[p + 1] = scale(p + 1)
            fired.append(True)
            if len(scan["ys"]) == nc and not scan.get("posted"):
                post()

        ops.append(operators(scaled[p], between))
    while len(scan["ys"]) < nc:
        scan_step()
    if not scan.get("posted"):
        post()

    for p in range(parts):
        cs = slice(p * nh, (p + 1) * nh)
        rq_ref[cs] = ops[p]["rq"]
        ya_ref[cs] = ops[p]["ya"]
        g2_ref[cs] = ops[p]["g2"]
        ha_ref[cs] = ops[p]["ha"]
        bonus_ref[p * th:(p + 1) * th, :] = scaled[p]["bonus"]


def _attn_kernel(q_ref, k_ref, vt_ref, o_ref, fast_ref, st_ref, acc_ref):
    nheads, seq, _ = q_ref.shape
    t = vt_ref.shape[2]
    i = pl.program_id(2)
    dn = (((1,), (1,)), ((), ()))
    sub = 8

    @pl.when(i == 0)
    def _():
        worst = []
        for hd in range(nheads):
            ka = jnp.abs(k_ref[hd].astype(F32)).reshape(seq // sub, sub, LANES)
            kmax = jnp.max(ka, axis=0).astype(BF16)
            bound = lax.dot_general(kmax, jnp.abs(q_ref[hd]), dn, preferred_element_type=F32)
            worst.append(jnp.max(bound) * ATTN_BOUND_SLACK)
        vmax = jnp.max(jnp.abs(vt_ref[...].astype(F32)))
        vlog = jnp.max(jnp.log2(jnp.maximum(jnp.full((sub, LANES), vmax), 1.0)))
        budget = F32_MAX_EXP2 - math.log2(seq)
        fast_ref[0] = (functools.reduce(jnp.maximum, worst) + vlog <= budget).astype(jnp.int32)

    fast_ok = fast_ref[0] == 1

    def scores(hd, jb):
        kb = k_ref[hd, pl.ds(pl.multiple_of(jb * t, t), t), :]
        qb = q_ref[hd, pl.ds(pl.multiple_of(i * t, t), t), :]
        return lax.dot_general(kb, qb, dn, preferred_element_type=F32)

    def causal(x):
        krow = lax.broadcasted_iota(jnp.int32, x.shape, 0)
        qcol = lax.broadcasted_iota(jnp.int32, x.shape, 1)
        return jnp.where(krow <= qcol, x, -jnp.inf)

    @pl.when(fast_ok)
    def _():
        ones_v = jnp.ones((ATTN_ONES_ROWS, t), BF16)
        acc_ref[...] = jnp.zeros_like(acc_ref)

        def produce(jb, slot):
            for hd in range(nheads):
                st_ref[slot, hd] = scores(hd, jb)

        def consume(jb, slot, masked):
            for hd in range(nheads):
                x = st_ref[slot, hd]
                p = jnp.exp2(causal(x) if masked else x).astype(BF16)
                vaug = jnp.concatenate(
                    [vt_ref[jb, hd * MLA_V:(hd + 1) * MLA_V, :], ones_v], axis=0)
                acc_ref[hd] += jnp.dot(vaug, p, preferred_element_type=F32)

        produce(0, 0)

        def pair(jj, c):
            j0 = 2 * jj
            produce(j0 + 1, 1)
            consume(j0, 0, False)
            produce(j0 + 2, 0)
            consume(j0 + 1, 1, False)
            return c

        lax.fori_loop(0, i // 2, pair, 0)

        @pl.when(i % 2 == 1)
        def _():
            produce(i, 1)
            consume(i - 1, 0, False)
            consume(i, 1, True)

        @pl.when(i % 2 == 0)
        def _():
            consume(i, 0, True)

        for hd in range(nheads):
            acc = acc_ref[hd]
            o_ref[hd * MLA_V:(hd + 1) * MLA_V, :] = (
                acc[:MLA_V] / acc[MLA_V:MLA_V + 1]).astype(o_ref.dtype)

    @pl.when(jnp.logical_not(fast_ok))
    def _():
        def block(jb, carry, masked):
            out = []
            for hd in range(nheads):
                m, l, acc = carry[hd]
                st = scores(hd, jb)
                if masked:
                    st = causal(st)
                mn = jnp.maximum(m, jnp.max(st, axis=0, keepdims=True))
                alpha = jnp.exp2(m - mn)
                p = jnp.exp2(st - mn)
                l = l * alpha + jnp.sum(p, axis=0, keepdims=True)
                vb = vt_ref[jb, hd * MLA_V:(hd + 1) * MLA_V, :]
                acc = acc * alpha + jnp.dot(vb, p.astype(BF16), preferred_element_type=F32)
                out.append((mn, l, acc))
            return tuple(out)

        init = tuple((jnp.full((1, t), -jnp.inf, F32), jnp.zeros((1, t), F32),
                      jnp.zeros((MLA_V, t), F32)) for _ in range(nheads))
        carry = lax.fori_loop(0, i, lambda jb, c: block(jb, c, False), init)
        carry = block(i, carry, True)
        for hd in range(nheads):
            _, l, acc = carry[hd]
            o_ref[hd * MLA_V:(hd + 1) * MLA_V, :] = (acc / l).astype(o_ref.dtype)


def _out_kernel(x_ref, yr_ref, ymt_ref, wor_ref, wom_ref, ln1_ref, w1_ref, w2_ref, ln2_ref, o_ref,
                *, alpha, ff_chunk, parts):
    rows = x_ref.shape[0] // parts
    part = [slice(p * rows, (p + 1) * rows) for p in range(parts)]
    mixes = []
    for sl in part:
        mix = jnp.dot(yr_ref[sl, :], wor_ref[...], preferred_element_type=F32)
        mixes.append(mix + lax.dot_general(ymt_ref[:, sl], wom_ref[...], (((0,), (0,)), ((), ())),
                                           preferred_element_type=F32))
    x1s = [_layer_norm(alpha * x_ref[sl, :] + mix, ln1_ref[0:1, :], ln1_ref[1:2, :])
           for sl, mix in zip(part, mixes)]
    for sl, x1 in zip(part, x1s):
        x1b = x1.astype(BF16)
        f = jnp.zeros_like(x1)
        for c in range(0, w1_ref.shape[1], ff_chunk):
            hid = jnp.dot(x1b, w1_ref[:, c:c + ff_chunk], preferred_element_type=F32)
            hid = jnp.square(jnp.maximum(hid, 0.0)).astype(BF16)
            f = f + jnp.dot(hid, w2_ref[c:c + ff_chunk, :], preferred_element_type=F32)
        o_ref[sl, :] = _layer_norm(alpha * x1 + f, ln2_ref[0:1, :], ln2_ref[1:2, :])


def _const_spec(shape):
    nd = len(shape)
    return pl.BlockSpec(shape, lambda *_: (0,) * nd, pipeline_mode=pl.Buffered(1))


def _pick(n, pref):
    t = min(pref, n)
    while n % t:
        t //= 2
    return t


def _layer(x, positions, w_in, shift_mu, decay_w0, decay_w2, iclr_a0, iclr_a2, gate_g2, k_k, k_a,
           r_k, lnx_g, lnx_b, q_norm_g, w_uq, kv_norm_g, w_ukv, w_out, ln1_g, ln1_b, w_ffn1,
           w_ffn2, ln2_g, ln2_b, alpha):
    B, S, D = x.shape
    T = B * S
    H = MLA_HEADS
    tm = _pick(S, 512)
    tb = _pick(S, 2 * tm)
    to = _pick(S, 2 * tm)
    chunk = _pick(tb, CHUNK)
    nt = S // tm

    w_r = w_in[:, :RWKV_COLS]
    w_m = w_in[:, RWKV_COLS:]
    w_pe = jnp.zeros((D, LANES), F32).at[:, PE_LO:PE_LO + MLA_ROPE].set(w_m[:, Q_LORA + KV_LORA:])
    w_all = jnp.concatenate([w_r, w_m[:, :Q_LORA + KV_LORA], w_pe], axis=1).astype(BF16)
    mu = shift_mu.reshape(1, RWKV_COLS)
    lora = jnp.zeros((LANES, 2 * RWKV_WIDTH), F32)
    lora = lora.at[:DECAY_LORA, :RWKV_WIDTH].set(decay_w2).at[DECAY_LORA:, RWKV_WIDTH:].set(iclr_a2)
    lora = lora.astype(BF16)
    w0a0 = jnp.concatenate([decay_w0, iclr_a0]).reshape(1, 2 * RWKV_WIDTH)
    g2 = gate_g2.astype(BF16)
    dq = MLA_NOPE + MLA_ROPE
    wq = jnp.zeros((Q_LORA, H, LANES), F32).at[:, :, :dq].set(w_uq.reshape(Q_LORA, H, dq))
    wq = wq.reshape(Q_LORA, H * LANES).astype(BF16)
    wkv = w_ukv.reshape(KV_LORA, H, MLA_NOPE + MLA_V)
    wk = jnp.zeros((KV_LORA, H, LANES), F32).at[:, :, :MLA_NOPE].set(wkv[:, :, :MLA_NOPE])
    wk = wk.reshape(KV_LORA, H * LANES).astype(BF16)
    wvt = wkv[:, :, MLA_NOPE:].reshape(KV_LORA, H * MLA_V).T.astype(BF16)
    inv_freq = ROPE_BASE ** (-jnp.arange(0, MLA_ROPE, 2, dtype=F32) / MLA_ROPE)
    freq = jnp.tile(inv_freq, LANES // PE_HALF).reshape(1, LANES)
    par = jnp.zeros((8, RWKV_WIDTH), F32)
    par = par.at[0].set(k_k).at[1].set(k_a).at[2].set(r_k.reshape(-1)).at[3].set(lnx_g).at[4].set(lnx_b)
    ln1 = jnp.stack([ln1_g, ln1_b])
    ln2 = jnp.stack([ln2_g, ln2_b])
    wo_r = w_out[:RWKV_WIDTH].astype(BF16)
    wo_m = w_out[RWKV_WIDTH:].astype(BF16)
    w1 = w_ffn1.astype(BF16)
    w2 = w_ffn2.astype(BF16)

    order = [(PE_LO // PE_HALF - b) % 8 for b in range(8)]
    pos_packed = jnp.repeat(positions.reshape(B, S // 8, 8)[:, :, order], PE_HALF, axis=-1)

    tok = lambda: pl.BlockSpec((None, tm, RWKV_WIDTH), lambda b, j: (b, j, 0))
    tok_shape = jax.ShapeDtypeStruct((B, S, RWKV_WIDTH), F32)
    head_spec = pl.BlockSpec((None, H, tm, LANES), lambda b, j: (b, 0, j, 0))
    r, k, v, lw, a, g, q, kh, vt = pl.pallas_call(
        functools.partial(_prep_kernel, scale=float(dq) ** -0.5 * math.log2(math.e),
                          parts=PREP_PARTS),
        grid=(B, nt),
        in_specs=[
            pl.BlockSpec((None, tm, D), lambda b, j: (b, j, 0)),
            pl.BlockSpec((None, tm // 8, LANES), lambda b, j: (b, j, 0)),
            _const_spec(w_all.shape), _const_spec(mu.shape), _const_spec(lora.shape),
            _const_spec(w0a0.shape), _const_spec(g2.shape),
            _const_spec((1, Q_LORA)), _const_spec(wq.shape),
            _const_spec((1, KV_LORA)), _const_spec(wk.shape), _const_spec(wvt.shape),
            _const_spec(freq.shape),
        ],
        out_specs=[tok(), tok(), tok(), tok(), tok(), tok(), head_spec, head_spec,
                   pl.BlockSpec((None, None, H * MLA_V, tm), lambda b, j: (b, j, 0, 0))],
        out_shape=[tok_shape] * 6 + [
            jax.ShapeDtypeStruct((B, H, S, LANES), BF16),
            jax.ShapeDtypeStruct((B, H, S, LANES), BF16),
            jax.ShapeDtypeStruct((B, nt, H * MLA_V, tm), BF16)],
        scratch_shapes=[pltpu.VMEM((8, RWKV_COLS), F32)],
        compiler_params=pltpu.CompilerParams(
            dimension_semantics=("arbitrary", "arbitrary"), vmem_limit_bytes=VMEM_LIMIT),
    )(x, pos_packed, w_all, mu, lora, w0a0, g2, q_norm_g.reshape(1, Q_LORA), wq,
      kv_norm_g.reshape(1, KV_LORA), wk, wvt, freq)

    npair = RWKV_WIDTH // LANES
    nsteps = S // tb
    nblocks = B * npair * nsteps
    nc = tb // chunk

    def block_of(s):
        return s // (npair * nsteps), s % nsteps, (s // nsteps) % npair

    cur = lambda: pl.BlockSpec((None, tb, LANES), lambda s: block_of(jnp.minimum(s, nblocks - 1)))
    prev = lambda: pl.BlockSpec((None, tb, LANES), lambda s: block_of(jnp.maximum(s - 1, 0)))
    y_rwkv = pl.pallas_call(
        functools.partial(_rwkv_kernel, chunk=chunk, nsteps=nsteps, parts=tb // tm),
        grid=(nblocks + 1,),
        in_specs=[cur(), cur(), cur(), cur(), cur(), prev(),
                  pl.BlockSpec((8, LANES), lambda s: (0, block_of(jnp.minimum(s, nblocks - 1))[2])),
                  pl.BlockSpec((8, LANES), lambda s: (0, block_of(jnp.maximum(s - 1, 0))[2]))],
        out_specs=prev(),
        out_shape=jax.ShapeDtypeStruct((B, S, RWKV_WIDTH), BF16),
        scratch_shapes=[pltpu.VMEM((LANES, LANES), F32),
                        pltpu.VMEM((nc, 2 * chunk, LANES), BF16),
                        pltpu.VMEM((nc, 2 * chunk, LANES), F32),
                        pltpu.VMEM((nc, LANES, LANES), BF16),
                        pltpu.VMEM((nc, LANES, LANES), F32),
                        pltpu.VMEM((tb, LANES), F32)],
        compiler_params=pltpu.CompilerParams(
            dimension_semantics=("arbitrary",), vmem_limit_bytes=VMEM_LIMIT),
    )(r, k, v, lw, a, g, par, par)

    y_mla_t = pl.pallas_call(
        _attn_kernel,
        grid=(B, H // ATTN_GROUP, nt),
        in_specs=[
            pl.BlockSpec((None, ATTN_GROUP, S, LANES), lambda b, h, i: (b, h, 0, 0)),
            pl.BlockSpec((None, ATTN_GROUP, S, LANES), lambda b, h, i: (b, h, 0, 0)),
            pl.BlockSpec((None, nt, ATTN_GROUP * MLA_V, tm), lambda b, h, i: (b, 0, h, 0)),
        ],
        out_specs=pl.BlockSpec((None, ATTN_GROUP * MLA_V, tm), lambda b, h, i: (b, h, i)),
        out_shape=jax.ShapeDtypeStruct((B, H * MLA_V, S), BF16),
        scratch_shapes=[pltpu.SMEM((1,), jnp.int32),
                        pltpu.VMEM((2, ATTN_GROUP, tm, tm), F32),
                        pltpu.VMEM((ATTN_GROUP, MLA_V + ATTN_ONES_ROWS, tm), F32)],
        compiler_params=pltpu.CompilerParams(
            dimension_semantics=("arbitrary", "arbitrary", "arbitrary"),
            vmem_limit_bytes=VMEM_LIMIT),
    )(q, kh, vt)

    out = pl.pallas_call(
        functools.partial(_out_kernel, alpha=alpha, ff_chunk=_pick(D_FF, 1024), parts=to // tm),
        grid=(B, S // to),
        in_specs=[
            pl.BlockSpec((None, to, D), lambda b, j: (b, j, 0)),
            pl.BlockSpec((None, to, RWKV_WIDTH), lambda b, j: (b, j, 0)),
            pl.BlockSpec((None, MLA_WIDTH, to), lambda b, j: (b, 0, j)),
            _const_spec(wo_r.shape), _const_spec(wo_m.shape), _const_spec(ln1.shape),
            _const_spec(w1.shape), _const_spec(w2.shape), _const_spec(ln2.shape),
        ],
        out_specs=pl.BlockSpec((None, to, D), lambda b, j: (b, j, 0)),
        out_shape=jax.ShapeDtypeStruct((B, S, D), F32),
        compiler_params=pltpu.CompilerParams(
            dimension_semantics=("arbitrary", "arbitrary"), vmem_limit_bytes=VMEM_LIMIT),
    )(x, y_rwkv, y_mla_t, wo_r, wo_m, ln1, w1, w2, ln2)
    return out


def kernel(x, positions, w_in, shift_mu, decay_w0, decay_w2, iclr_a0, iclr_a2, gate_g2, k_k, k_a, r_k, lnx_g, lnx_b, q_norm_g, w_uq, kv_norm_g, w_ukv, w_out, ln1_g, ln1_b, w_ffn1, w_ffn2, ln2_g, ln2_b):
    depth = w_in.shape[0]
    alpha = (2 * depth) ** 0.25
    params = (w_in, shift_mu, decay_w0, decay_w2, iclr_a0, iclr_a2, gate_g2, k_k, k_a, r_k, lnx_g,
              lnx_b, q_norm_g, w_uq, kv_norm_g, w_ukv, w_out, ln1_g, ln1_b, w_ffn1, w_ffn2,
              ln2_g, ln2_b)
    for l in range(depth):
        x = _layer(x, positions, *(p[l] for p in params), alpha)
    return x
```

```python
import functools
import math

import jax
import jax.numpy as jnp
from jax import lax
from jax.experimental import pallas as pl
from jax.experimental.pallas import tpu as pltpu

F32 = jnp.float32
BF16 = jnp.bfloat16

D_MODEL = 1024
RWKV_HEAD = 64
RWKV_HEADS = 8
RWKV_WIDTH = RWKV_HEADS * RWKV_HEAD
DECAY_LORA = 64
ICLR_LORA = 64
GATE_LORA = 128
MLA_HEADS = 8
MLA_NOPE = 64
MLA_ROPE = 32
MLA_V = 64
MLA_WIDTH = MLA_HEADS * MLA_V
Q_LORA = 384
KV_LORA = 256
ROPE_BASE = 10000.0
D_FF = 4 * D_MODEL
LN_EPS = 1e-5
RMS_EPS = 1e-6
GN_EPS = 64e-5
RWKV_COLS = 3 * RWKV_WIDTH + DECAY_LORA + ICLR_LORA + GATE_LORA
MLA_COLS = Q_LORA + KV_LORA + MLA_ROPE

LANES = 128
VMEM_LIMIT = 56 * 1024 * 1024

C_R, C_K, C_V = 0, RWKV_WIDTH, 2 * RWKV_WIDTH
C_WA = 3 * RWKV_WIDTH
C_G = C_WA + DECAY_LORA + ICLR_LORA
C_Q = RWKV_COLS
C_KV = C_Q + Q_LORA
C_PE = C_KV + KV_LORA
IN_COLS_PAD = C_PE + LANES
PE_LO = MLA_NOPE
PE_HALF = MLA_ROPE // 2

PREP_PARTS = 2
CHUNK = 64
ATTN_GROUP = 4
ATTN_ONES_ROWS = 16
F32_MAX_EXP2 = 126.0
ATTN_BOUND_SLACK = 1.01


def _dot(a, b):
    return jnp.dot(a.astype(BF16), b.astype(BF16), preferred_element_type=F32)


def _split_bf16(x):
    hi = x.astype(BF16)
    lo = (x - hi.astype(F32)).astype(BF16)
    return hi, lo


def _sigmoid(z):
    return 1.0 / (1.0 + jnp.exp(-z))


def _layer_norm(x, g, b):
    mu = jnp.mean(x, axis=-1, keepdims=True)
    d = x - mu
    var = jnp.mean(d * d, axis=-1, keepdims=True)
    return d * lax.rsqrt(var + LN_EPS) * g + b


def _prep_kernel(x_ref, pos_ref, w_ref, mu_ref, lora_ref, w0a0_ref, g2_ref, qg_ref, wq_ref,
                 kvg_ref, wk_ref, wvt_ref, freq_ref,
                 r_ref, k_ref, v_ref, lw_ref, a_ref, g_ref, q_ref, kh_ref, vt_ref, carry_ref,
                 *, scale, parts):
    tm = x_ref.shape[0]
    j = pl.program_id(1)

    @pl.when(j == 0)
    def _():
        carry_ref[...] = jnp.zeros_like(carry_ref)

    rows = tm // parts
    part = [slice(p * rows, (p + 1) * rows) for p in range(parts)]
    hs = [jnp.dot(x_ref[sl, :].astype(BF16), w_ref[...], preferred_element_type=F32)
          for sl in part]

    row = lax.broadcasted_iota(jnp.int32, (rows, RWKV_COLS), 0)
    lane = lax.broadcasted_iota(jnp.int32, (rows, LANES), 1)
    first_half = (lane >= PE_LO) & (lane < PE_LO + PE_HALF)

    def rms(c, gain):
        return c * lax.rsqrt(jnp.mean(c * c, axis=-1, keepdims=True) + RMS_EPS) * gain

    second_half = (lane >= PE_LO + PE_HALF) & (lane < PE_LO + MLA_ROPE)
    packed = rows // 8
    angs = [pos_ref[p * packed:(p + 1) * packed, :].astype(F32) * freq_ref[...]
            for p in range(parts)]

    def own_lanes(t):
        x = jnp.broadcast_to(t[:, None, :], (packed, 8, LANES)).reshape(rows, LANES)
        return pltpu.roll(x, 0, 1, stride=PE_HALF, stride_axis=0)

    prev_row = carry_ref[0:1, :]
    for p, (rs, h) in enumerate(zip(part, hs)):
        hr = h[:, :RWKV_COLS]
        prev = jnp.where(row == 0, prev_row, pltpu.roll(hr, 1, axis=0))
        prev_row = hr[rows - 1:rows, :]
        hr = hr + (prev - hr) * mu_ref[...]

        r_ref[rs, :] = hr[:, C_R:C_R + RWKV_WIDTH]
        k_ref[rs, :] = hr[:, C_K:C_K + RWKV_WIDTH]
        v_ref[rs, :] = hr[:, C_V:C_V + RWKV_WIDTH]

        xwa = hr[:, C_WA:C_WA + LANES]
        xwa = jnp.where(lane < DECAY_LORA, jnp.tanh(xwa), xwa)
        z = _dot(xwa, lora_ref[...]) + w0a0_ref[...]
        sg = _sigmoid(z)
        lw_ref[rs, :] = (-math.exp(-0.5)) * sg[:, :RWKV_WIDTH]
        a_ref[rs, :] = sg[:, RWKV_WIDTH:]
        g_ref[rs, :] = _dot(_sigmoid(hr[:, C_G:C_G + GATE_LORA]), g2_ref[...])

        cqn = rms(h[:, C_Q:C_Q + Q_LORA], qg_ref[...]).astype(BF16)
        ckvn = rms(h[:, C_KV:C_KV + KV_LORA], kvg_ref[...]).astype(BF16)
        q_all = jnp.dot(cqn, wq_ref[...], preferred_element_type=F32)
        k_all = jnp.dot(ckvn, wk_ref[...], preferred_element_type=F32)
        vt = lax.dot_general(wvt_ref[...], ckvn, (((1,), (1,)), ((), ())),
                             preferred_element_type=F32)
        vt_ref[:, rs] = vt.astype(vt_ref.dtype)

        cos_own, sin_own = own_lanes(jnp.cos(angs[p])), own_lanes(jnp.sin(angs[p]))
        cosf = jnp.where(first_half, cos_own,
                         jnp.where(second_half, pltpu.roll(cos_own, PE_HALF, axis=1), 1.0))
        sins = jnp.where(first_half, -sin_own,
                         jnp.where(second_half, pltpu.roll(sin_own, PE_HALF, axis=1), 0.0))

        def rope(zg):
            rot = jnp.where(first_half, pltpu.roll(zg, LANES - PE_HALF, axis=1),
                            pltpu.roll(zg, PE_HALF, axis=1))
            return zg * cosf + rot * sins

        kpe = rope(h[:, C_PE:C_PE + LANES])
        for hd in range(MLA_HEADS):
            sl = slice(hd * LANES, (hd + 1) * LANES)
            q_ref[hd, rs, :] = (rope(q_all[:, sl]) * scale).astype(q_ref.dtype)
            kh_ref[hd, rs, :] = (k_all[:, sl] + kpe).astype(kh_ref.dtype)
    carry_ref[0:1, :] = prev_row


_NN = (((2,), (1,)), ((0,), (0,)))
_NT = (((2,), (2,)), ((0,), (0,)))
_TN = (((1,), (1,)), ((0,), (0,)))


def _rwkv_kernel(r_ref, k_ref, v_ref, lw_ref, a_ref, gprev_ref, par_ref, parprev_ref, o_ref,
                 h_ref, rq_ref, ya_ref, g2_ref, ha_ref, bonus_ref, *, chunk, nsteps, parts):
    tb = r_ref.shape[0]
    C = chunk
    nc = tb // C
    C2 = 2 * C
    s = pl.program_id(0)

    @pl.when(s == 0)
    def _():
        for ref in (h_ref, rq_ref, ya_ref, g2_ref, ha_ref, bonus_ref):
            ref[...] = jnp.zeros_like(ref)

    first_of_sequence = (jnp.maximum(s - 1, 0) % nsteps) == 0
    scan = {"h": h_ref[...] * jnp.where(first_of_sequence, 0.0, 1.0), "ys": []}

    def scan_step():
        c = len(scan["ys"])
        if c < nc:
            hb = scan["h"].astype(BF16)
            y2 = jnp.dot(rq_ref[c], hb, preferred_element_type=F32) + ya_ref[c]
            scan["ys"].append(y2[:C] + y2[C:])
            scan["h"] = jnp.dot(g2_ref[c], hb, preferred_element_type=F32) + ha_ref[c]

    lane = lax.broadcasted_iota(jnp.int32, (1, LANES), 1)
    m0 = (lane < RWKV_HEAD).astype(F32)
    m1 = 1.0 - m0
    m0b, m1b = m0.astype(BF16), m1.astype(BF16)
    rr = lax.broadcasted_iota(jnp.int32, (LANES, LANES), 0)
    cc = lax.broadcasted_iota(jnp.int32, (LANES, LANES), 1)
    ones2 = ((rr < RWKV_HEAD) == (cc < RWKV_HEAD)).astype(BF16)
    eye = (rr == cc).astype(F32)
    bdot = functools.partial(lax.dot_general, preferred_element_type=F32)

    k_k, k_a, r_k = par_ref[0:1, :], par_ref[1:2, :], par_ref[2:3, :]
    ln_g, ln_b = parprev_ref[3:4, :], parprev_ref[4:5, :]

    nh = nc // parts
    th = tb // parts

    def post():
        h_ref[...] = scan["h"]
        y = jnp.concatenate(scan["ys"], axis=0)
        inv_n = 1.0 / RWKV_HEAD
        mu = _dot(y, ones2) * inv_n
        d = y - mu
        var = _dot(d * d, ones2) * inv_n
        yn = d * lax.rsqrt(var + GN_EPS) * ln_g + ln_b
        o_ref[...] = ((yn + bonus_ref[...]) * gprev_ref[...]).astype(o_ref.dtype)
        scan["posted"] = True

    def scale(p):
        rs = slice(p * th, (p + 1) * th)
        r, k, v, lw, a = r_ref[rs, :], k_ref[rs, :], v_ref[rs, :], lw_ref[rs, :], a_ref[rs, :]
        kk0 = k * k_k
        kk = kk0 / jnp.maximum(jnp.sqrt(_dot(kk0 * kk0, ones2)), 1e-12)
        kmod = k * (1.0 + (a - 1.0) * k_a)
        bonus = _dot(r * kmod * r_k, ones2) * v
        bb = kk * a

        to3 = lambda t: t.reshape(nh, C, LANES)
        r3, k3, v3, lw3, kk3, bb3 = to3(r), to3(kmod), to3(v), to3(lw), to3(kk), to3(bb)

        ti = lax.broadcasted_iota(jnp.int32, (nh, C, C), 1)
        si = lax.broadcasted_iota(jnp.int32, (nh, C, C), 2)
        tril = (si <= ti).astype(BF16)
        lhi, llo = _split_bf16(lw3)
        cum = bdot(tril, lhi, _NN) + bdot(tril, llo, _NN)
        cend = cum[:, C - 1:C, :]
        p_inv = jnp.exp(-cum)
        p_end = jnp.exp(cend - cum)

        def expand(t):
            tbf = t.astype(BF16)
            return jnp.concatenate([tbf * m0b, tbf * m1b], axis=1)

        rt = r3 * jnp.exp(cum)
        at2 = expand(-kk3 * jnp.exp(cum - lw3))
        return dict(
            rt=rt, at2=at2, v2=expand(v3), p_all=jnp.exp(cend), bonus=bonus,
            lhs=jnp.concatenate([at2, expand(rt)], axis=1),
            rhs=jnp.concatenate([expand(bb3 * p_inv), expand(k3 * p_inv)], axis=1),
            bk=jnp.concatenate([expand(bb3 * p_end), expand(k3 * p_end)], axis=1))

    def operators(d, between):
        m_all = bdot(d["lhs"], d["rhs"], _NT)
        between()
        ri = lax.broadcasted_iota(jnp.int32, (4 * C, 4 * C), 0)
        ci = lax.broadcasted_iota(jnp.int32, (4 * C, 4 * C), 1)
        bottom = (ri >= C2).astype(jnp.int32)
        right = (ci >= C2).astype(jnp.int32)
        keep = (ri - C2 * bottom) - (ci - C2 * right) + bottom > 0
        m_all = jnp.where(keep[None], m_all, 0.0)
        a_ab = m_all[:, :C2, :C2]
        a_ak = m_all[:, :C2, C2:].astype(BF16)
        a_r = m_all[:, C2:, :].astype(BF16)

        rounds = max(1, (C - 1).bit_length()) - 1
        tmat = eye[None] + a_ab
        pw = a_ab.astype(BF16)
        pw = bdot(pw, pw, _NN).astype(BF16)
        between()
        for rnd in range(rounds):
            if rnd + 1 < rounds:
                both = bdot(pw, jnp.concatenate([tmat.astype(BF16), pw], axis=2), _NN)
                tmat = tmat + both[:, :, :C2]
                pw = both[:, :, C2:].astype(BF16)
            else:
                tmat = tmat + bdot(pw, tmat.astype(BF16), _NN)
            between()

        v2 = d["v2"]
        akv = bdot(a_ak, v2, _NN).astype(BF16)
        between()
        wu = bdot(tmat.astype(BF16), jnp.concatenate([d["at2"], akv], axis=2), _NN)
        wu = wu.astype(BF16)
        between()
        zmat = jnp.concatenate(
            [wu, jnp.concatenate([jnp.zeros_like(v2), v2], axis=2)], axis=1)
        ry = bdot(a_r, zmat, _NN)
        gh = bdot(d["bk"], zmat, _TN)
        rt = d["rt"]
        return dict(
            rq=(jnp.concatenate([rt * m0, rt * m1], axis=1) + ry[:, :, :LANES]).astype(BF16),
            ya=ry[:, :, LANES:],
            g2=(gh[:, :, :LANES] + eye[None] * d["p_all"]).astype(BF16),
            ha=gh[:, :, LANES:])

    scaled = {0: scale(0)}
    ops = []
    for p in range(parts):
        fired = []

        def between(p=p, fired=fired):
            scan_step()
            if not fired and p + 1 < parts:
                scaled[p + 1] = scale(p + 1)
            fired.append(True)
            if len(scan["ys"]) == nc and not scan.get("posted"):
                post()

        ops.append(operators(scaled[p], between))
    while len(scan["ys"]) < nc:
        scan_step()
    if not scan.get("posted"):
        post()

    for p in range(parts):
        cs = slice(p * nh, (p + 1) * nh)
        rq_ref[cs] = ops[p]["rq"]
        ya_ref[cs] = ops[p]["ya"]
        g2_ref[cs] = ops[p]["g2"]
        ha_ref[cs] = ops[p]["ha"]
        bonus_ref[p * th:(p + 1) * th, :] = scaled[p]["bonus"]


def _attn_kernel(q_ref, k_ref, vt_ref, o_ref, fast_ref, st_ref, acc_ref):
    nheads, seq, _ = q_ref.shape
    t = vt_ref.shape[2]
    i = pl.program_id(2)
    dn = (((1,), (1,)), ((), ()))
    sub = 8

    @pl.when(i == 0)
    def _():
        worst = []
        for hd in range(nheads):
            ka = jnp.abs(k_ref[hd].astype(F32)).reshape(seq // sub, sub, LANES)
            kmax = jnp.max(ka, axis=0).astype(BF16)
            bound = lax.dot_general(kmax, jnp.abs(q_ref[hd]), dn, preferred_element_type=F32)
            worst.append(jnp.max(bound) * ATTN_BOUND_SLACK)
        vmax = jnp.max(jnp.abs(vt_ref[...].astype(F32)))
        vlog = jnp.max(jnp.log2(jnp.maximum(jnp.full((sub, LANES), vmax), 1.0)))
        budget = F32_MAX_EXP2 - math.log2(seq)
        fast_ref[0] = (functools.reduce(jnp.maximum, worst) + vlog <= budget).astype(jnp.int32)

    fast_ok = fast_ref[0] == 1

    def scores(hd, jb):
        kb = k_ref[hd, pl.ds(pl.multiple_of(jb * t, t), t), :]
        qb = q_ref[hd, pl.ds(pl.multiple_of(i * t, t), t), :]
        return lax.dot_general(kb, qb, dn, preferred_element_type=F32)

    def causal(x):
        krow = lax.broadcasted_iota(jnp.int32, x.shape, 0)
        qcol = lax.broadcasted_iota(jnp.int32, x.shape, 1)
        return jnp.where(krow <= qcol, x, -jnp.inf)

    @pl.when(fast_ok)
    def _():
        ones_v = jnp.ones((ATTN_ONES_ROWS, t), BF16)
        acc_ref[...] = jnp.zeros_like(acc_ref)

        def produce_head(jb, slot, hd):
            st_ref[slot, hd] = scores(hd, jb)

        def consume_head(jb, slot, masked, hd):
            x = st_ref[slot, hd]
            p = jnp.exp2(causal(x) if masked else x).astype(BF16)
            vaug = jnp.concatenate(
                [vt_ref[jb, hd * MLA_V:(hd + 1) * MLA_V, :], ones_v], axis=0)
            acc_ref[hd] += jnp.dot(vaug, p, preferred_element_type=F32)

        def produce(jb, slot):
            for hd in range(nheads):
                produce_head(jb, slot, hd)

        def consume(jb, slot, masked):
            for hd in range(nheads):
                consume_head(jb, slot, masked, hd)

        produce(0, 0)

        def pair(jj, c):
            j0 = 2 * jj
            for hd in range(nheads):
                produce_head(j0 + 1, 1, hd)
                consume_head(j0, 0, False, hd)
            for hd in range(nheads):
                produce_head(j0 + 2, 0, hd)
                consume_head(j0 + 1, 1, False, hd)
            return c

        lax.fori_loop(0, i // 2, pair, 0)

        @pl.when(i % 2 == 1)
        def _():
            for hd in range(nheads):
                produce_head(i, 1, hd)
                consume_head(i - 1, 0, False, hd)
            consume(i, 1, True)

        @pl.when(i % 2 == 0)
        def _():
            consume(i, 0, True)

        for hd in range(nheads):
            acc = acc_ref[hd]
            o_ref[hd * MLA_V:(hd + 1) * MLA_V, :] = (
                acc[:MLA_V] / acc[MLA_V:MLA_V + 1]).astype(o_ref.dtype)

    @pl.when(jnp.logical_not(fast_ok))
    def _():
        def block(jb, carry, masked):
            out = []
            for hd in range(nheads):
                m, l, acc = carry[hd]
                st = scores(hd, jb)
                if masked:
                    st = causal(st)
                mn = jnp.maximum(m, jnp.max(st, axis=0, keepdims=True))
                alpha = jnp.exp2(m - mn)
                p = jnp.exp2(st - mn)
                l = l * alpha + jnp.sum(p, axis=0, keepdims=True)
                vb = vt_ref[jb, hd * MLA_V:(hd + 1) * MLA_V, :]
                acc = acc * alpha + jnp.dot(vb, p.astype(BF16), preferred_element_type=F32)
                out.append((mn, l, acc))
            return tuple(out)

        init = tuple((jnp.full((1, t), -jnp.inf, F32), jnp.zeros((1, t), F32),
                      jnp.zeros((MLA_V, t), F32)) for _ in range(nheads))
        carry = lax.fori_loop(0, i, lambda jb, c: block(jb, c, False), init)
        carry = block(i, carry, True)
        for hd in range(nheads):
            _, l, acc = carry[hd]
            o_ref[hd * MLA_V:(hd + 1) * MLA_V, :] = (acc / l).astype(o_ref.dtype)


def _out_kernel(x_ref, yr_ref, ymt_ref, wor_ref, wom_ref, ln1_ref, w1_ref, w2_ref, ln2_ref, o_ref,
                *, alpha, ff_chunk, parts):
    rows = x_ref.shape[0] // parts
    part = [slice(p * rows, (p + 1) * rows) for p in range(parts)]
    mixes = []
    for sl in part:
        mix = jnp.dot(yr_ref[sl, :], wor_ref[...], preferred_element_type=F32)
        mixes.append(mix + lax.dot_general(ymt_ref[:, sl], wom_ref[...], (((0,), (0,)), ((), ())),
                                           preferred_element_type=F32))
    x1s = [_layer_norm(alpha * x_ref[sl, :] + mix, ln1_ref[0:1, :], ln1_ref[1:2, :])
           for sl, mix in zip(part, mixes)]
    for sl, x1 in zip(part, x1s):
        x1b = x1.astype(BF16)
        f = jnp.zeros_like(x1)
        for c in range(0, w1_ref.shape[1], ff_chunk):
            hid = jnp.dot(x1b, w1_ref[:, c:c + ff_chunk], preferred_element_type=F32)
            hid = jnp.square(jnp.maximum(hid, 0.0)).astype(BF16)
            f = f + jnp.dot(hid, w2_ref[c:c + ff_chunk, :], preferred_element_type=F32)
        o_ref[sl, :] = _layer_norm(alpha * x1 + f, ln2_ref[0:1, :], ln2_ref[1:2, :])


def _const_spec(shape):
    nd = len(shape)
    return pl.BlockSpec(shape, lambda *_: (0,) * nd, pipeline_mode=pl.Buffered(1))


def _pick(n, pref):
    t = min(pref, n)
    while n % t:
        t //= 2
    return t


def _layer(x, positions, w_in, shift_mu, decay_w0, decay_w2, iclr_a0, iclr_a2, gate_g2, k_k, k_a,
           r_k, lnx_g, lnx_b, q_norm_g, w_uq, kv_norm_g, w_ukv, w_out, ln1_g, ln1_b, w_ffn1,
           w_ffn2, ln2_g, ln2_b, alpha):
    B, S, D = x.shape
    T = B * S
    H = MLA_HEADS
    tm = _pick(S, 512)
    tb = _pick(S, 2 * tm)
    to = _pick(S, 2 * tm)
    chunk = _pick(tb, CHUNK)
    nt = S // tm

    w_r = w_in[:, :RWKV_COLS]
    w_m = w_in[:, RWKV_COLS:]
    w_pe = jnp.zeros((D, LANES), F32).at[:, PE_LO:PE_LO + MLA_ROPE].set(w_m[:, Q_LORA + KV_LORA:])
    w_all = jnp.concatenate([w_r, w_m[:, :Q_LORA + KV_LORA], w_pe], axis=1).astype(BF16)
    mu = shift_mu.reshape(1, RWKV_COLS)
    lora = jnp.zeros((LANES, 2 * RWKV_WIDTH), F32)
    lora = lora.at[:DECAY_LORA, :RWKV_WIDTH].set(decay_w2).at[DECAY_LORA:, RWKV_WIDTH:].set(iclr_a2)
    lora = lora.astype(BF16)
    w0a0 = jnp.concatenate([decay_w0, iclr_a0]).reshape(1, 2 * RWKV_WIDTH)
    g2 = gate_g2.astype(BF16)
    dq = MLA_NOPE + MLA_ROPE
    wq = jnp.zeros((Q_LORA, H, LANES), F32).at[:, :, :dq].set(w_uq.reshape(Q_LORA, H, dq))
    wq = wq.reshape(Q_LORA, H * LANES).astype(BF16)
    wkv = w_ukv.reshape(KV_LORA, H, MLA_NOPE + MLA_V)
    wk = jnp.zeros((KV_LORA, H, LANES), F32).at[:, :, :MLA_NOPE].set(wkv[:, :, :MLA_NOPE])
    wk = wk.reshape(KV_LORA, H * LANES).astype(BF16)
    wvt = wkv[:, :, MLA_NOPE:].reshape(KV_LORA, H * MLA_V).T.astype(BF16)
    inv_freq = ROPE_BASE ** (-jnp.arange(0, MLA_ROPE, 2, dtype=F32) / MLA_ROPE)
    freq = jnp.tile(inv_freq, LANES // PE_HALF).reshape(1, LANES)
    par = jnp.zeros((8, RWKV_WIDTH), F32)
    par = par.at[0].set(k_k).at[1].set(k_a).at[2].set(r_k.reshape(-1)).at[3].set(lnx_g).at[4].set(lnx_b)
    ln1 = jnp.stack([ln1_g, ln1_b])
    ln2 = jnp.stack([ln2_g, ln2_b])
    wo_r = w_out[:RWKV_WIDTH].astype(BF16)
    wo_m = w_out[RWKV_WIDTH:].astype(BF16)
    w1 = w_ffn1.astype(BF16)
    w2 = w_ffn2.astype(BF16)

    order = [(PE_LO // PE_HALF - b) % 8 for b in range(8)]
    pos_packed = jnp.repeat(positions.reshape(B, S // 8, 8)[:, :, order], PE_HALF, axis=-1)

    tok = lambda: pl.BlockSpec((None, tm, RWKV_WIDTH), lambda b, j: (b, j, 0))
    tok_shape = jax.ShapeDtypeStruct((B, S, RWKV_WIDTH), F32)
    head_spec = pl.BlockSpec((None, H, tm, LANES), lambda b, j: (b, 0, j, 0))
    r, k, v, lw, a, g, q, kh, vt = pl.pallas_call(
        functools.partial(_prep_kernel, scale=float(dq) ** -0.5 * math.log2(math.e),
                          parts=PREP_PARTS),
        grid=(B, nt),
        in_specs=[
            pl.BlockSpec((None, tm, D), lambda b, j: (b, j, 0)),
            pl.BlockSpec((None, tm // 8, LANES), lambda b, j: (b, j, 0)),
            _const_spec(w_all.shape), _const_spec(mu.shape), _const_spec(lora.shape),
            _const_spec(w0a0.shape), _const_spec(g2.shape),
            _const_spec((1, Q_LORA)), _const_spec(wq.shape),
            _const_spec((1, KV_LORA)), _const_spec(wk.shape), _const_spec(wvt.shape),
            _const_spec(freq.shape),
        ],
        out_specs=[tok(), tok(), tok(), tok(), tok(), tok(), head_spec, head_spec,
                   pl.BlockSpec((None, None, H * MLA_V, tm), lambda b, j: (b, j, 0, 0))],
        out_shape=[tok_shape] * 6 + [
            jax.ShapeDtypeStruct((B, H, S, LANES), BF16),
            jax.ShapeDtypeStruct((B, H, S, LANES), BF16),
            jax.ShapeDtypeStruct((B, nt, H * MLA_V, tm), BF16)],
        scratch_shapes=[pltpu.VMEM((8, RWKV_COLS), F32)],
        compiler_params=pltpu.CompilerParams(
            dimension_semantics=("arbitrary", "arbitrary"), vmem_limit_bytes=VMEM_LIMIT),
    )(x, pos_packed, w_all, mu, lora, w0a0, g2, q_norm_g.reshape(1, Q_LORA), wq,
      kv_norm_g.reshape(1, KV_LORA), wk, wvt, freq)

    npair = RWKV_WIDTH // LANES
    nsteps = S // tb
    nblocks = B * npair * nsteps
    nc = tb // chunk

    def block_of(s):
        return s // (npair * nsteps), s % nsteps, (s // nsteps) % npair

    cur = lambda: pl.BlockSpec((None, tb, LANES), lambda s: block_of(jnp.minimum(s, nblocks - 1)))
    prev = lambda: pl.BlockSpec((None, tb, LANES), lambda s: block_of(jnp.maximum(s - 1, 0)))
    y_rwkv = pl.pallas_call(
        functools.partial(_rwkv_kernel, chunk=chunk, nsteps=nsteps, parts=tb // tm),
        grid=(nblocks + 1,),
        in_specs=[cur(), cur(), cur(), cur(), cur(), prev(),
                  pl.BlockSpec((8, LANES), lambda s: (0, block_of(jnp.minimum(s, nblocks - 1))[2])),
                  pl.BlockSpec((8, LANES), lambda s: (0, block_of(jnp.maximum(s - 1, 0))[2]))],
        out_specs=prev(),
        out_shape=jax.ShapeDtypeStruct((B, S, RWKV_WIDTH), BF16),
        scratch_shapes=[pltpu.VMEM((LANES, LANES), F32),
                        pltpu.VMEM((nc, 2 * chunk, LANES), BF16),
                        pltpu.VMEM((nc, 2 * chunk, LANES), F32),
                        pltpu.VMEM((nc, LANES, LANES), BF16),
                        pltpu.VMEM((nc, LANES, LANES), F32),
                        pltpu.VMEM((tb, LANES), F32)],
        compiler_params=pltpu.CompilerParams(
            dimension_semantics=("arbitrary",), vmem_limit_bytes=VMEM_LIMIT),
    )(r, k, v, lw, a, g, par, par)

    y_mla_t = pl.pallas_call(
        _attn_kernel,
        grid=(B, H // ATTN_GROUP, nt),
        in_specs=[
            pl.BlockSpec((None, ATTN_GROUP, S, LANES), lambda b, h, i: (b, h, 0, 0)),
            pl.BlockSpec((None, ATTN_GROUP, S, LANES), lambda b, h, i: (b, h, 0, 0)),
            pl.BlockSpec((None, nt, ATTN_GROUP * MLA_V, tm), lambda b, h, i: (b, 0, h, 0)),
        ],
        out_specs=pl.BlockSpec((None, ATTN_GROUP * MLA_V, tm), lambda b, h, i: (b, h, i)),
        out_shape=jax.ShapeDtypeStruct((B, H * MLA_V, S), BF16),
        scratch_shapes=[pltpu.SMEM((1,), jnp.int32),
                        pltpu.VMEM((2, ATTN_GROUP, tm, tm), F32),
                        pltpu.VMEM((ATTN_GROUP, MLA_V + ATTN_ONES_ROWS, tm), F32)],
        compiler_params=pltpu.CompilerParams(
            dimension_semantics=("arbitrary", "arbitrary", "arbitrary"),
            vmem_limit_bytes=VMEM_LIMIT),
    )(q, kh, vt)

    out = pl.pallas_call(
        functools.partial(_out_kernel, alpha=alpha, ff_chunk=_pick(D_FF, 1024), parts=to // tm),
        grid=(B, S // to),
        in_specs=[
            pl.BlockSpec((None, to, D), lambda b, j: (b, j, 0)),
            pl.BlockSpec((None, to, RWKV_WIDTH), lambda b, j: (b, j, 0)),
            pl.BlockSpec((None, MLA_WIDTH, to), lambda b, j: (b, 0, j)),
            _const_spec(wo_r.shape), _const_spec(wo_m.shape), _const_spec(ln1.shape),
            _const_spec(w1.shape), _const_spec(w2.shape), _const_spec(ln2.shape),
        ],
        out_specs=pl.BlockSpec((None, to, D), lambda b, j: (b, j, 0)),
        out_shape=jax.ShapeDtypeStruct((B, S, D), F32),
        compiler_params=pltpu.CompilerParams(
            dimension_semantics=("arbitrary", "arbitrary"), vmem_limit_bytes=VMEM_LIMIT),
    )(x, y_rwkv, y_mla_t, wo_r, wo_m, ln1, w1, w2, ln2)
    return out


def kernel(x, positions, w_in, shift_mu, decay_w0, decay_w2, iclr_a0, iclr_a2, gate_g2, k_k, k_a, r_k, lnx_g, lnx_b, q_norm_g, w_uq, kv_norm_g, w_ukv, w_out, ln1_g, ln1_b, w_ffn1, w_ffn2, ln2_g, ln2_b):
    depth = w_in.shape[0]
    alpha = (2 * depth) ** 0.25
    params = (w_in, shift_mu, decay_w0, decay_w2, iclr_a0, iclr_a2, gate_g2, k_k, k_a, r_k, lnx_g,
              lnx_b, q_norm_g, w_uq, kv_norm_g, w_ukv, w_out, ln1_g, ln1_b, w_ffn1, w_ffn2,
              ln2_g, ln2_b)
    for l in range(depth):
        x = _layer(x, positions, *(p[l] for p in params), alpha)
    return x
```
